```python
import math
import jax, jax.numpy as jnp
from jax import lax
import numpy as np

D_MODEL = 1024
BATCH = 8
SEQ = 4096
DEPTH = 4

CHUNK = 64
QBLK = 128
PLE_DIM = 256
N_BUCKETS = 32
MAX_DISTANCE = 256
ROPE_THETA = 10000.0
LN_EPS = 1e-5
RMS_EPS = 1e-6

H_A = 8
A_NOPE = 64
A_ROPE = 32
A_V = 64
A_Q_RANK = 512
A_KV_RANK = 256
A_SCALE = (A_NOPE + A_ROPE) ** -0.5
H_B = 8
B_DH = 32
B_SCALE = B_DH ** -0.5
H_C = 16
KV_C = 4
C_GROUP = H_C // KV_C
C_DH = 64
C_SCALE = C_DH ** -0.5
H_IDX = 16
IDX_DH = 64
IDX_ROPE = 32
IDX_SCALE = IDX_DH ** -0.5
TOPK_MAX = 256
P_HEADS = 8
N_KEYS = 128
N_EXPERTS = N_KEYS * N_KEYS
P_DQ = 256
P_DHALF = P_DQ // 2
P_TOPK = 16
P_BLK = 128

EVEN_MIX = H_A * A_V + H_B * 2 * B_DH
ODD_MIX = H_C * C_DH
EVEN_SPLITS = (A_Q_RANK, A_KV_RANK, A_ROPE, H_B * 2 * B_DH, H_B * 2 * B_DH, H_B * 2 * B_DH)
ODD_SPLITS = (H_C * C_DH, KV_C * C_DH, KV_C * C_DH, H_IDX * IDX_DH, IDX_DH, H_IDX)
N_EVEN = (DEPTH + 1) // 2
N_ODD = DEPTH // 2
DN_ALPHA = (2 * DEPTH) ** 0.25
DN_BETA = (8 * DEPTH) ** -0.25

kernel_name = 'hybrid_mla_diff_dsa_peer_encoder'


def split_cols(a, sizes):
    out, start = [], 0
    for s in sizes:
        out.append(a[..., start:start + s])
        start += s
    return out


def layer_norm(x, g, b):
    xf = x.astype(jnp.float32)
    mu = jnp.mean(xf, axis=-1, keepdims=True)
    xc = xf - mu
    var = jnp.mean(xc * xc, axis=-1, keepdims=True)
    return (xc * lax.rsqrt(var + LN_EPS) * g.astype(jnp.float32) + b.astype(jnp.float32)).astype(x.dtype)


def rms_norm(x, g):
    xf = x.astype(jnp.float32)
    y = xf * lax.rsqrt(jnp.mean(xf * xf, axis=-1, keepdims=True) + RMS_EPS)
    return (y * g.astype(jnp.float32)).astype(x.dtype)


def rope(x, pos):
    half = x.shape[-1] // 2
    freqs = ROPE_THETA ** (-jnp.arange(half, dtype=jnp.float32) / half)
    ang = pos.astype(jnp.float32)[..., None] * freqs
    ang = ang.reshape(ang.shape[:2] + (1,) * (x.ndim - 3) + (half,))
    cos, sin = jnp.cos(ang), jnp.sin(ang)
    x1 = x[..., :half].astype(jnp.float32)
    x2 = x[..., half:].astype(jnp.float32)
    return jnp.concatenate([x1 * cos - x2 * sin, x2 * cos + x1 * sin], axis=-1).astype(x.dtype)


def rope_partial(x, pos):
    return jnp.concatenate([rope(x[..., :IDX_ROPE], pos), x[..., IDX_ROPE:]], axis=-1)


def t5_bucket(rel):
    nb = N_BUCKETS // 2
    max_exact = nb // 2
    n = jnp.abs(rel)
    large = max_exact + (jnp.log(jnp.maximum(n, 1).astype(jnp.float32) / max_exact)
                         / math.log(MAX_DISTANCE / max_exact) * (nb - max_exact)).astype(jnp.int32)
    large = jnp.minimum(large, nb - 1)
    return jnp.where(rel > 0, nb, 0) + jnp.where(n < max_exact, n, large)


def to_blocks(a):
    b, s = a.shape[:2]
    return jnp.moveaxis(a.reshape((b, s // QBLK, QBLK) + a.shape[2:]), 1, 0)


def from_blocks(a):
    a = jnp.moveaxis(a, 0, 1)
    return a.reshape((a.shape[0], a.shape[1] * a.shape[2]) + a.shape[3:])


def even_mixer(h, positions, w_in, w_uq, w_ukv, q_norm_g, kv_norm_g,
               lam_q1, lam_k1, lam_q2, lam_k2, subln_g, w_o, bias_tab, lam_init):
    b, s, _ = h.shape
    c_q, c_kv, k_r, q_d, k_d, v_d = split_cols(h @ w_in, EVEN_SPLITS)
    q_a = (rms_norm(c_q, q_norm_g) @ w_uq).reshape(b, s, H_A, A_NOPE + A_ROPE)
    q_nope, q_rope = q_a[..., :A_NOPE], rope(q_a[..., A_NOPE:], positions)
    kv_a = (rms_norm(c_kv, kv_norm_g) @ w_ukv).reshape(b, s, H_A, A_NOPE + A_V)
    k_nope, v_a = kv_a[..., :A_NOPE], kv_a[..., A_NOPE:]
    k_rope = rope(k_r, positions)
    q_d = q_d.reshape(b, s, H_B, 2, B_DH)
    k_d = k_d.reshape(b, s, H_B, 2, B_DH)
    v_d = v_d.reshape(b, s, H_B, 2 * B_DH)
    lam = (jnp.exp(jnp.sum(lam_q1 * lam_k1, dtype=jnp.float32))
           - jnp.exp(jnp.sum(lam_q2 * lam_k2, dtype=jnp.float32)) + lam_init)
    key_chunk = jnp.arange(s) // CHUNK

    def block(args):
        qn, qr, qd, qpos, bi = args
        q_chunk = (bi * QBLK + jnp.arange(QBLK)) // CHUNK
        allowed = key_chunk[None, :] <= q_chunk[:, None]
        s_a = (jnp.einsum('bqhd,bkhd->bhqk', qn, k_nope)
               + jnp.einsum('bqhr,bkr->bhqk', qr, k_rope)) * A_SCALE
        p_a = jax.nn.softmax(jnp.where(allowed, s_a.astype(jnp.float32), -jnp.inf), axis=-1)
        o_a = jnp.einsum('bhqk,bkhd->bqhd', p_a.astype(v_a.dtype), v_a)
        bucket = t5_bucket(positions[:, None, :] - qpos[:, :, None])
        bias = jnp.moveaxis(bias_tab[bucket], -1, 1).astype(jnp.float32)
        s_d = jnp.einsum('bqhmd,bkhmd->bhmqk', qd, k_d).astype(jnp.float32) * B_SCALE + bias[:, :, None]
        p_d = jax.nn.softmax(jnp.where(allowed, s_d, -jnp.inf), axis=-1)
        w_d = p_d[:, :, 0] - lam * p_d[:, :, 1]
        o_d = jnp.einsum('bhqk,bkhd->bqhd', w_d.astype(v_d.dtype), v_d)
        o_d = rms_norm(o_d, subln_g) * (1.0 - lam_init)
        return jnp.concatenate([o_a.reshape(b, QBLK, H_A * A_V),
                                o_d.reshape(b, QBLK, H_B * 2 * B_DH)], axis=-1)

    nb = s // QBLK
    o = lax.map(block, (to_blocks(q_nope), to_blocks(q_rope), to_blocks(q_d),
                        to_blocks(positions), jnp.arange(nb)))
    return from_blocks(o) @ w_o


def odd_mixer(h, positions, w_in, w_o, bias_tab):
    b, s, _ = h.shape
    q, k, v, qi, ki, wi = split_cols(h @ w_in, ODD_SPLITS)
    q = q.reshape(b, s, KV_C, C_GROUP, C_DH)
    k = k.reshape(b, s, KV_C, C_DH)
    v = v.reshape(b, s, KV_C, C_DH)
    qi = rope_partial(qi.reshape(b, s, H_IDX, IDX_DH), positions)
    ki = rope_partial(ki, positions)
    wi = wi * (H_IDX ** -0.5)
    n_sel = min(TOPK_MAX, s // 4)
    key_chunk = jnp.arange(s) // CHUNK
    gather = jax.vmap(lambda a, idx: a[idx])

    def block(args):
        qb, qib, wib, qpos, bi = args
        q_chunk = (bi * QBLK + jnp.arange(QBLK)) // CHUNK
        allowed = key_chunk[None, :] <= q_chunk[:, None]
        idx_logits = jnp.einsum('bqhd,bkd->bqhk', qib, ki) * IDX_SCALE
        score = jnp.einsum('bqh,bqhk->bqk', wib, jax.nn.relu(idx_logits)).astype(jnp.float32)
        top_s, sel = lax.top_k(jnp.where(allowed, score, -jnp.inf), n_sel)
        valid = jnp.isfinite(top_s)
        kg = gather(k, sel)
        vg = gather(v, sel)
        rel = gather(positions, sel) - qpos[:, :, None]
        bias = bias_tab[t5_bucket(rel)].reshape(b, QBLK, n_sel, KV_C, C_GROUP)
        bias = jnp.transpose(bias, (0, 1, 3, 4, 2)).astype(jnp.float32)
        sc = jnp.einsum('bqkgd,bqnkd->bqkgn', qb, kg).astype(jnp.float32) * C_SCALE + bias
        pr = jax.nn.softmax(jnp.where(valid[:, :, None, None, :], sc, -jnp.inf), axis=-1)
        o = jnp.einsum('bqkgn,bqnkd->bqkgd', pr.astype(vg.dtype), vg)
        return o.reshape(b, QBLK, H_C * C_DH)

    nb = s // QBLK
    o = lax.map(block, (to_blocks(q), to_blocks(qi), to_blocks(wi),
                        to_blocks(positions), jnp.arange(nb)))
    return from_blocks(o) @ w_o


def peer(h, w_q, sub_k1, sub_k2, u_tab, v_tab):
    b, s, d = h.shape
    t = b * s
    xt = h.reshape(t, d)
    q = (xt @ w_q).reshape(t, P_HEADS, 2, P_DHALF)
    s1 = jnp.einsum('thd,nd->thn', q[:, :, 0], sub_k1).astype(jnp.float32)
    s2 = jnp.einsum('thd,nd->thn', q[:, :, 1], sub_k2).astype(jnp.float32)
    t1, i1 = lax.top_k(s1, P_TOPK)
    t2, i2 = lax.top_k(s2, P_TOPK)
    cand = (t1[..., :, None] + t2[..., None, :]).reshape(t, P_HEADS, P_TOPK * P_TOPK)
    cidx = (i1[..., :, None] * N_KEYS + i2[..., None, :]).reshape(t, P_HEADS, P_TOPK * P_TOPK)
    top_s, pick = lax.top_k(cand, P_TOPK)
    eidx = jnp.take_along_axis(cidx, pick, axis=-1).reshape(t, P_HEADS * P_TOPK)
    g = jax.nn.softmax(top_s, axis=-1).reshape(t, P_HEADS * P_TOPK).astype(h.dtype)
    nb = t // P_BLK

    def expert_block(args):
        xb, eb, gb = args
        hb = jnp.einsum('tkd,td->tk', u_tab[eb], xb)
        return jnp.einsum('tk,tkd->td', gb * jax.nn.gelu(hb), v_tab[eb])

    out = lax.map(expert_block, (xt.reshape(nb, P_BLK, d),
                                 eidx.reshape(nb, P_BLK, P_HEADS * P_TOPK),
                                 g.reshape(nb, P_BLK, P_HEADS * P_TOPK)))
    return out.reshape(b, s, d)


def setup_inputs(seed: int = 0) -> dict:
    key = jax.random.key(seed)
    ks = iter(jax.random.split(key, 32))

    def nrm(shape, scale):
        return jax.random.normal(next(ks), shape, jnp.float32) * scale

    offs = jax.random.randint(next(ks), (BATCH,), 0, 64, dtype=jnp.int32) * CHUNK
    positions = offs[:, None] + jnp.arange(SEQ, dtype=jnp.int32)[None, :]
    ev_in = sum(EVEN_SPLITS)
    od_in = sum(ODD_SPLITS)
    return {
        'x': nrm((BATCH, SEQ, D_MODEL), 1.0),
        'p': nrm((DEPTH, BATCH, SEQ, PLE_DIM), 1.0),
        'positions': positions,
        'rel_bias': nrm((N_BUCKETS, H_B + H_C), 0.2),
        'ev_w_in': nrm((N_EVEN, D_MODEL, ev_in), D_MODEL ** -0.5),
        'ev_w_uq': nrm((N_EVEN, A_Q_RANK, H_A * (A_NOPE + A_ROPE)), A_Q_RANK ** -0.5),
        'ev_w_ukv': nrm((N_EVEN, A_KV_RANK, H_A * (A_NOPE + A_V)), A_KV_RANK ** -0.5),
        'ev_q_norm': 1.0 + nrm((N_EVEN, A_Q_RANK), 0.02),
        'ev_kv_norm': 1.0 + nrm((N_EVEN, A_KV_RANK), 0.02),
        'ev_lam_q1': nrm((N_EVEN, B_DH), 0.1),
        'ev_lam_k1': nrm((N_EVEN, B_DH), 0.1),
        'ev_lam_q2': nrm((N_EVEN, B_DH), 0.1),
        'ev_lam_k2': nrm((N_EVEN, B_DH), 0.1),
        'ev_subln': 1.0 + nrm((N_EVEN, 2 * B_DH), 0.02),
        'ev_w_o': nrm((N_EVEN, EVEN_MIX, D_MODEL), DN_BETA * EVEN_MIX ** -0.5),
        'od_w_in': nrm((N_ODD, D_MODEL, od_in), D_MODEL ** -0.5),
        'od_w_o': nrm((N_ODD, ODD_MIX, D_MODEL), DN_BETA * ODD_MIX ** -0.5),
        'ln1_g': 1.0 + nrm((DEPTH, D_MODEL), 0.02),
        'ln1_b': nrm((DEPTH, D_MODEL), 0.02),
        'ln2_g': 1.0 + nrm((DEPTH, D_MODEL), 0.02),
        'ln2_b': nrm((DEPTH, D_MODEL), 0.02),
        'peer_w_q': nrm((DEPTH, D_MODEL, P_HEADS * P_DQ), D_MODEL ** -0.5),
        'peer_k1': nrm((DEPTH, N_KEYS, P_DHALF), P_DHALF ** -0.5),
        'peer_k2': nrm((DEPTH, N_KEYS, P_DHALF), P_DHALF ** -0.5),
        'peer_u': nrm((DEPTH, N_EXPERTS, D_MODEL), D_MODEL ** -0.5),
        'peer_v': nrm((DEPTH, N_EXPERTS, D_MODEL), DN_BETA * P_HEADS ** -0.5),
        'ple_w': nrm((DEPTH, PLE_DIM, D_MODEL), DN_BETA * PLE_DIM ** -0.5),
        'ple_gate_w': nrm((DEPTH, D_MODEL, D_MODEL), D_MODEL ** -0.5),
        'ple_gate_b': nrm((DEPTH, D_MODEL), 0.02),
    }


def reference(x, p, positions, rel_bias, ev_w_in, ev_w_uq, ev_w_ukv, ev_q_norm, ev_kv_norm,
              ev_lam_q1, ev_lam_k1, ev_lam_q2, ev_lam_k2, ev_subln, ev_w_o, od_w_in, od_w_o,
              ln1_g, ln1_b, ln2_g, ln2_b, peer_w_q, peer_k1, peer_k2, peer_u, peer_v,
              ple_w, ple_gate_w, ple_gate_b):
    bias_b = rel_bias[:, :H_B]
    bias_c = rel_bias[:, H_B:]
    h = x
    for i in range(DEPTH):
        j = i // 2
        if i % 2 == 0:
            lam_init = 0.8 - 0.6 * math.exp(-0.3 * i)
            mix = even_mixer(h, positions, ev_w_in[j], ev_w_uq[j], ev_w_ukv[j], ev_q_norm[j],
                             ev_kv_norm[j], ev_lam_q1[j], ev_lam_k1[j], ev_lam_q2[j], ev_lam_k2[j],
                             ev_subln[j], ev_w_o[j], bias_b, lam_init)
        else:
            mix = odd_mixer(h, positions, od_w_in[j], od_w_o[j], bias_c)
        h = layer_norm(DN_ALPHA * h + mix, ln1_g[i], ln1_b[i])
        ffn = peer(h, peer_w_q[i], peer_k1[i], peer_k2[i], peer_u[i], peer_v[i])
        h = layer_norm(DN_ALPHA * h + ffn, ln2_g[i], ln2_b[i])
        gate = jax.nn.sigmoid(h @ ple_gate_w[i] + ple_gate_b[i])
        h = h + gate * (p[i] @ ple_w[i])
    return h
```

```python
import functools
import math

import numpy as np
import jax
import jax.numpy as jnp
from jax import lax
from jax.experimental import pallas as pl
from jax.experimental.pallas import tpu as pltpu

D_MODEL = 1024
DEPTH = 4
CHUNK = 64
PLE_DIM = 256
N_BUCKETS = 32
MAX_DISTANCE = 256
ROPE_THETA = 10000.0
LN_EPS = 1e-5
RMS_EPS = 1e-6
H_A, A_NOPE, A_ROPE, A_V, A_Q_RANK, A_KV_RANK = 8, 64, 32, 64, 512, 256
A_SCALE = (A_NOPE + A_ROPE) ** -0.5
H_B, B_DH = 8, 32
B_SCALE = B_DH ** -0.5
H_C, KV_C, C_DH = 16, 4, 64
C_GROUP = H_C // KV_C
C_SCALE = C_DH ** -0.5
H_IDX, IDX_DH, IDX_ROPE = 16, 64, 32
IDX_SCALE = IDX_DH ** -0.5
TOPK_MAX = 256
P_HEADS, N_KEYS, P_DQ, P_TOPK = 8, 128, 256, 16
P_DHALF = P_DQ // 2
N_EXPERTS = N_KEYS * N_KEYS
DN_ALPHA = (2 * DEPTH) ** 0.25

LANES = 128
ATT_BLK = 256
PEER_TE = 8 * N_KEYS
VMEM_LIMIT = 52 * 1024 * 1024

BF16 = jnp.bfloat16
F32 = jnp.float32
NEG_INF = float("-inf")
M_INIT = -1e30
INT_MIN = np.int32(-2 ** 31)


def _cparams(*sem):
    return pltpu.CompilerParams(dimension_semantics=sem, vmem_limit_bytes=VMEM_LIMIT)


def _full(shape):
    n = len(shape)
    return pl.BlockSpec(shape, lambda *_: (0,) * n)


def _smem():
    return pl.BlockSpec(memory_space=pltpu.SMEM)


def _rms(v, g):
    return v * lax.rsqrt(jnp.mean(v * v, axis=-1, keepdims=True) + RMS_EPS) * g


def _layer_norm(y, g, b):
    mu = jnp.mean(y, axis=-1, keepdims=True)
    yc = y - mu
    var = jnp.mean(yc * yc, axis=-1, keepdims=True)
    return yc * lax.rsqrt(var + LN_EPS) * g + b


def _dot(a, b):
    return jnp.dot(a, b, preferred_element_type=F32)


def _dot_nt(a, b):
    return lax.dot_general(a, b, (((1,), (1,)), ((), ())), preferred_element_type=F32)


def _even_proj_body(h_ref, cos_ref, sin_ref, w_in_ref, w_uq_ref, w_ukv_ref, qg_ref, kvg_ref,
                    qn_ref, qr_ref, kn_ref, kr_ref, va_ref, qd_ref, kd_ref, vd_ref):
    x = h_ref[0].astype(BF16)
    acc = _dot(x, w_in_ref[...])
    cos = cos_ref[0]
    sin = sin_ref[0]
    cq = _rms(acc[:, 0:512], qg_ref[...]).astype(BF16)
    qa = _dot(cq, w_uq_ref[...])
    qn_ref[0] = (qa[:, 0:512] * A_SCALE).astype(BF16)
    cos2 = jnp.concatenate([cos, cos], axis=1)
    sin2 = jnp.concatenate([sin, sin], axis=1)
    qr_ref[0] = ((qa[:, 512:768] * cos2 + qa[:, 768:1024] * sin2) * A_SCALE).astype(BF16)
    ckv = _rms(acc[:, 512:768], kvg_ref[...]).astype(BF16)
    kv = _dot(ckv, w_ukv_ref[...])
    kn_ref[0] = kv[:, 0:512].astype(BF16)
    va_ref[0] = kv[:, 512:1024].astype(BF16)
    qd_ref[0] = (acc[:, 768:1280] * B_SCALE).astype(BF16)
    kd_ref[0] = acc[:, 1280:1792].astype(BF16)
    vd_ref[0] = acc[:, 1792:2304].astype(BF16)
    kr_ref[0] = (acc[:, 2304:2432] * cos + acc[:, 2432:2560] * sin).astype(BF16)


def _even_proj(h, cos, sin, w_in, w_uq, w_ukv, qg, kvg):
    b, s, d = h.shape
    tm = min(512, s)
    tok = lambda w: pl.BlockSpec((1, tm, w), lambda i, j: (i, j, 0))
    widths = (512, 256, 512, 128, 512, 512, 512, 512)
    return pl.pallas_call(
        _even_proj_body,
        grid=(b, s // tm),
        in_specs=[tok(d), tok(LANES), tok(LANES), _full(w_in.shape), _full(w_uq.shape),
                  _full(w_ukv.shape), _full(qg.shape), _full(kvg.shape)],
        out_specs=[tok(w) for w in widths],
        out_shape=[jax.ShapeDtypeStruct((b, s, w), BF16) for w in widths],
        compiler_params=_cparams("parallel", "parallel"),
        name="even_proj",
    )(h, cos, sin, w_in, w_uq, w_ukv, qg, kvg)


def _softmax_step(state, s, v):
    m, l, acc = state
    m_new = jnp.maximum(m, jnp.max(s, axis=1, keepdims=True))
    alpha = jnp.exp(m - m_new)
    p = jnp.exp(s - m_new)
    l_new = alpha * l + jnp.sum(p, axis=1, keepdims=True)
    acc_new = alpha * acc + _dot(p.astype(BF16), v)
    return m_new, l_new, acc_new


def _init_state(tq):
    return (jnp.full((tq, 1), M_INIT, F32), jnp.zeros((tq, 1), F32), jnp.zeros((tq, LANES), F32))


def _chunk_allowed(tq, tk):
    r = lax.broadcasted_iota(jnp.int32, (tq, tk), 0) // CHUNK
    c = lax.broadcasted_iota(jnp.int32, (tq, tk), 1) // CHUNK
    return c <= r


def _even_attn_body(lam_ref, bfar_ref, qn_ref, qr_ref, kn_ref, kr_ref, va_ref, qd_ref, kd_ref, vd_ref,
                    bias_ref, sg_ref, oa_ref, od_ref, *, blk, lam_init):
    p = pl.program_id(1)
    qi = pl.program_id(2)
    lane = lax.broadcasted_iota(jnp.int32, (blk, LANES), 1)
    qn = qn_ref[0]
    qr = qr_ref[0]
    qd = qd_ref[0]
    zero = jnp.zeros_like(qn)
    streams = []
    for hh in range(2):
        lo = 64 * hh
        r0 = 32 * (2 * (p % 2) + hh)
        qn_m = jnp.where((lane >= lo) & (lane < lo + 64), qn, zero)
        qr_m = jnp.where((lane >= r0) & (lane < r0 + 32), qr, zero)
        streams.append((jnp.concatenate([qn_m, qr_m], axis=1), "a", hh))
        streams.append((jnp.where((lane >= lo) & (lane < lo + 32), qd, zero), "d", hh))
        streams.append((jnp.where((lane >= lo + 32) & (lane < lo + 64), qd, zero), "d", hh))
    allowed = _chunk_allowed(blk, blk)

    def block(j, states, near):
        ks = pl.multiple_of(j * blk, blk)
        ka = jnp.concatenate([kn_ref[0, pl.ds(ks, blk), :], kr_ref[0, pl.ds(ks, blk), :]], axis=1)
        kd = kd_ref[0, pl.ds(ks, blk), :]
        va = va_ref[0, pl.ds(ks, blk), :]
        vd = vd_ref[0, pl.ds(ks, blk), :]
        out = []
        for (q, kind, hh), st in zip(streams, states):
            if kind == "a":
                s = _dot_nt(q, ka)
            else:
                s = _dot_nt(q, kd)
                if near is None:
                    s = s + bfar_ref[2 * p + hh]
                else:
                    s = s + bias_ref[0, hh, near]
            if near == 1:
                s = jnp.where(qi >= 1, s, NEG_INF)
            if near == 0:
                s = jnp.where(allowed, s, NEG_INF)
            out.append(_softmax_step(st, s, va if kind == "a" else vd))
        return tuple(out)

    states = tuple(_init_state(blk) for _ in streams)
    states = lax.fori_loop(0, jnp.maximum(qi - 1, 0), lambda j, st: block(j, st, None), states)
    states = block(jnp.maximum(qi - 1, 0), states, 1)
    states = block(qi, states, 0)

    lam = lam_ref[0]
    first = lane < 64
    o = [acc / l for (_, l, acc) in states]
    oa_ref[0] = jnp.where(first, o[0], o[3]).astype(BF16)
    od = jnp.where(first, o[1] - lam * o[2], o[4] - lam * o[5])
    sq = od * od
    ms0 = jnp.sum(jnp.where(first, sq, 0.0), axis=1, keepdims=True)
    ms1 = jnp.sum(jnp.where(first, 0.0, sq), axis=1, keepdims=True)
    ms = jnp.where(first, ms0, ms1) * (1.0 / (2 * B_DH))
    od_ref[0] = (od * lax.rsqrt(ms + RMS_EPS) * sg_ref[...] * (1.0 - lam_init)).astype(BF16)


def _even_attn(lam, bfar, qn, qr, kn, kr, va, qd, kd, vd, bias, sg, lam_init):
    b, s, _ = qn.shape
    blk = ATT_BLK
    qblk = lambda f: pl.BlockSpec((1, blk, LANES), f)
    kv = lambda f: pl.BlockSpec((1, s, LANES), f)
    body = functools.partial(_even_attn_body, blk=blk, lam_init=lam_init)
    return pl.pallas_call(
        body,
        grid=(b, H_A // 2, s // blk),
        in_specs=[_smem(), _smem(),
                  qblk(lambda i, p, q: (i, q, p)), qblk(lambda i, p, q: (i, q, p // 2)),
                  kv(lambda i, p, q: (i, 0, p)), kv(lambda i, p, q: (i, 0, 0)), kv(lambda i, p, q: (i, 0, p)),
                  qblk(lambda i, p, q: (i, q, p)), kv(lambda i, p, q: (i, 0, p)), kv(lambda i, p, q: (i, 0, p)),
                  pl.BlockSpec((1, 2, 2, blk, blk), lambda i, p, q: (p, 0, 0, 0, 0)),
                  _full(sg.shape)],
        out_specs=[qblk(lambda i, p, q: (i, q, p)), qblk(lambda i, p, q: (i, q, p))],
        out_shape=[jax.ShapeDtypeStruct((b, s, 512), BF16)] * 2,
        compiler_params=_cparams("parallel", "parallel", "arbitrary"),
        name="even_attn",
    )(lam, bfar, qn, qr, kn, kr, va, qd, kd, vd, bias, sg)


def _oproj_ln_body(*refs, n_in):
    o_refs = refs[:n_in]
    w_refs = refs[n_in:2 * n_in]
    h_ref, g_ref, b_ref, out_ref = refs[2 * n_in:]
    mix = _dot(o_refs[0][...], w_refs[0][...])
    for o_ref, w_ref in zip(o_refs[1:], w_refs[1:]):
        mix = mix + _dot(o_ref[...], w_ref[...])
    out_ref[...] = _layer_norm(DN_ALPHA * h_ref[...] + mix, g_ref[...], b_ref[...])


def _oproj_ln(os_, ws, h, g, b):
    t, d = h.shape
    tm = min(512, t)
    tok = lambda w: pl.BlockSpec((tm, w), lambda i: (i, 0))
    return pl.pallas_call(
        functools.partial(_oproj_ln_body, n_in=len(os_)),
        grid=(t // tm,),
        in_specs=[tok(o.shape[1]) for o in os_] + [_full(w.shape) for w in ws]
                 + [tok(d), _full(g.shape), _full(b.shape)],
        out_specs=tok(d),
        out_shape=jax.ShapeDtypeStruct((t, d), F32),
        compiler_params=_cparams("parallel"),
        name="oproj_ln",
    )(*os_, *ws, h, g, b)


def _odd_proj_body(h_ref, cos_ref, sin_ref, w_ref, q_ref, k_ref, v_ref, qi_ref, ki_ref, wi_ref):
    x = h_ref[0].astype(BF16)
    acc = _dot(x, w_ref[...])
    cos = cos_ref[0]
    sin = sin_ref[0]
    q_ref[0] = (acc[:, 0:1024] * C_SCALE).astype(BF16)
    k_ref[0] = acc[:, 1024:1280].astype(BF16)
    v_ref[0] = acc[:, 1280:1536].astype(BF16)
    cos8 = jnp.concatenate([cos] * 8, axis=1)
    sin8 = jnp.concatenate([sin] * 8, axis=1)
    qi_ref[0] = ((acc[:, 1536:2560] * cos8 + acc[:, 2560:3584] * sin8) * IDX_SCALE).astype(BF16)
    ki_ref[0] = (acc[:, 3584:3712] * cos + acc[:, 3712:3840] * sin).astype(BF16)
    wi_ref[0] = acc[:, 3840:3968] * (H_IDX ** -0.5)


def _odd_proj(h, cos, sin, w):
    b, s, d = h.shape
    tm = min(256, s)
    tok = lambda wd: pl.BlockSpec((1, tm, wd), lambda i, j: (i, j, 0))
    widths = (1024, 256, 256, 1024, 128, 128)
    dts = (BF16, BF16, BF16, BF16, BF16, F32)
    return pl.pallas_call(
        _odd_proj_body,
        grid=(b, s // tm),
        in_specs=[tok(d), tok(LANES), tok(LANES), _full(w.shape)],
        out_specs=[tok(wd) for wd in widths],
        out_shape=[jax.ShapeDtypeStruct((b, s, wd), dt) for wd, dt in zip(widths, dts)],
        compiler_params=_cparams("parallel", "parallel"),
        name="odd_proj",
    )(h, cos, sin, w)


def _dsa_body(bfar_ref, qi_ref, ki_ref, wi_ref, q_ref, k_ref, v_ref, bias_ref, o_ref,
              key_ref, madd_ref, wrep_ref, *, blk, n_sel):
    qb = pl.program_id(1)
    hb = pl.program_id(2)
    lane = lax.broadcasted_iota(jnp.int32, (blk, LANES), 1)
    first = lane < 64
    chunk_mask = jnp.where(_chunk_allowed(blk, blk), 0.0, NEG_INF)
    nt = blk // LANES

    @pl.when(hb == 0)
    def _select():
        wi = wi_ref[0]
        for h in range(H_IDX):
            wrep_ref[h] = jnp.broadcast_to(wi[:, h:h + 1], (blk, LANES))
        qis = []
        for hp in range(H_IDX // 2):
            qp = qi_ref[0, :, hp * LANES:(hp + 1) * LANES]
            qis.append(jnp.where(first, qp, jnp.zeros_like(qp)))
            qis.append(jnp.where(first, jnp.zeros_like(qp), qp))

        def score(j, _):
            ks = pl.multiple_of(j * blk, blk)
            kib = ki_ref[0, pl.ds(ks, blk), :]
            sc = jnp.zeros((blk, blk), F32)
            for h in range(H_IDX):
                w = jnp.concatenate([wrep_ref[h]] * nt, axis=1)
                sc = sc + jnp.maximum(_dot_nt(qis[h], kib), 0.0) * w
            sc = sc + jnp.where(j < qb, 0.0, chunk_mask)
            bits = pltpu.bitcast(sc, jnp.int32)
            key_ref[j] = jnp.where(bits < 0, bits ^ np.int32(0x7FFFFFFF), bits)
            return 0

        lax.fori_loop(0, qb + 1, score, 0)

        def count_ge(cand):
            def body(j, part):
                kj = key_ref[j]
                for t in range(nt):
                    part = part + jnp.where(kj[:, t * LANES:(t + 1) * LANES] >= cand, 1.0, 0.0)
                return part
            part = lax.fori_loop(0, qb + 1, body, jnp.zeros((blk, LANES), F32))
            return jnp.broadcast_to(jnp.sum(part, axis=1, keepdims=True), (blk, LANES))

        n = float(n_sel)
        zero_i = jnp.zeros((blk, LANES), jnp.int32)
        ans = jnp.where(count_ge(zero_i) >= n, zero_i, zero_i + INT_MIN)

        def bit_step(t, ans):
            cand = ans | lax.shift_left(jnp.int32(1), 30 - t)
            return jnp.where(count_ge(cand) >= n, cand, ans)

        thr = lax.fori_loop(0, 31, bit_step, ans)

        def make_mask(j, _):
            kj = key_ref[j]
            sel = jnp.concatenate([jnp.where(kj[:, t * LANES:(t + 1) * LANES] >= thr, 0.0, NEG_INF)
                                   for t in range(nt)], axis=1)
            madd_ref[j] = sel + jnp.where(j < qb, 0.0, chunk_mask)
            return 0

        lax.fori_loop(0, qb + 1, make_mask, 0)

    q = q_ref[0]
    zq = jnp.zeros_like(q)
    qs = (jnp.where(first, q, zq), jnp.where(first, zq, q))

    def block(j, states, near):
        ks = pl.multiple_of(j * blk, blk)
        kb = k_ref[0, pl.ds(ks, blk), :]
        vb = v_ref[0, pl.ds(ks, blk), :]
        ma = madd_ref[j]
        out = []
        for hh in range(2):
            s = _dot_nt(qs[hh], kb) + ma
            if near is None:
                s = s + bfar_ref[2 * hb + hh]
            else:
                s = s + bias_ref[0, hh, near]
            if near == 1:
                s = jnp.where(qb >= 1, s, NEG_INF)
            out.append(_softmax_step(states[hh], s, vb))
        return tuple(out)

    states = (_init_state(blk), _init_state(blk))
    states = lax.fori_loop(0, jnp.maximum(qb - 1, 0), lambda j, st: block(j, st, None), states)
    states = block(jnp.maximum(qb - 1, 0), states, 1)
    states = block(qb, states, 0)
    o = [acc / l for (_, l, acc) in states]
    o_ref[0] = jnp.where(first, o[0], o[1]).astype(BF16)


def _dsa_attn(bfar, qi, ki, wi, q, k, v, bias, n_sel):
    b, s, _ = q.shape
    blk = ATT_BLK
    nk = s // blk
    nhb = H_C // 2
    body = functools.partial(_dsa_body, blk=blk, n_sel=n_sel)
    return pl.pallas_call(
        body,
        grid=(b, nk, nhb),
        in_specs=[_smem(),
                  pl.BlockSpec((1, blk, 1024), lambda i, qb, hb: (i, qb, 0)),
                  pl.BlockSpec((1, s, LANES), lambda i, qb, hb: (i, 0, 0)),
                  pl.BlockSpec((1, blk, LANES), lambda i, qb, hb: (i, qb, 0)),
                  pl.BlockSpec((1, blk, LANES), lambda i, qb, hb: (i, qb, hb)),
                  pl.BlockSpec((1, s, LANES), lambda i, qb, hb: (i, 0, hb // C_GROUP)),
                  pl.BlockSpec((1, s, LANES), lambda i, qb, hb: (i, 0, hb // C_GROUP)),
                  pl.BlockSpec((1, 2, 2, blk, blk), lambda i, qb, hb: (hb, 0, 0, 0, 0))],
        out_specs=pl.BlockSpec((1, blk, LANES), lambda i, qb, hb: (i, qb, hb)),
        out_shape=jax.ShapeDtypeStruct((b, s, H_C * C_DH), BF16),
        scratch_shapes=[pltpu.VMEM((nk, blk, blk), jnp.int32),
                        pltpu.VMEM((nk, blk, blk), F32),
                        pltpu.VMEM((H_IDX, blk, LANES), F32)],
        compiler_params=_cparams("parallel", "arbitrary", "arbitrary"),
        name="dsa_attn",
    )(bfar, qi, ki, wi, q, k, v, bias)


def _top_rows(cur, n):
    vals = []
    for _ in range(n):
        m = jnp.max(cur, axis=0, keepdims=True)
        vals.append(m)
        cur = jnp.where(cur == m, NEG_INF, cur)
    return vals


def _peer_route_body(h_ref, wq_ref, k1_ref, k2_ref, s2_ref, e2_ref, th_ref, e1_ref, q_scr):
    tm = h_ref.shape[0]
    q = _dot(h_ref[...].astype(BF16), wq_ref[...]).astype(BF16)
    for h in range(P_HEADS):
        q_scr[h] = q[:, h * P_DQ:(h + 1) * P_DQ]
    row16 = lax.broadcasted_iota(jnp.int32, (P_TOPK, tm), 0)

    def head(h, _):
        qh = q_scr[h]
        s1 = _dot_nt(k1_ref[...], qh[:, 0:P_DHALF])
        s2 = _dot_nt(k2_ref[...], qh[:, P_DHALF:P_DQ])
        t1 = _top_rows(s1, P_TOPK + 1)
        t2 = _top_rows(s2, P_TOPK + 1)
        t2m = jnp.zeros((P_TOPK, tm), F32)
        for r in range(P_TOPK):
            t2m = jnp.where(row16 == r, t2[r], t2m)
        cand = jnp.concatenate([t1[r] + t2m for r in range(P_TOPK)], axis=0)
        c = _top_rows(cand, P_TOPK + 1)
        c17 = jnp.maximum(c[P_TOPK], jnp.maximum(t1[P_TOPK] + t2[0], t1[0] + t2[P_TOPK]))
        tau = 0.5 * (c[P_TOPK - 1] + c17)
        z = jnp.sum(jnp.where(cand >= tau, jnp.exp(cand - (t1[0] + t2[0])), 0.0), axis=0, keepdims=True)
        s2_ref[h] = s2
        e2_ref[h] = jnp.exp(s2 - t2[0])
        th_ref[h] = tau - s1
        e1_ref[h] = jnp.exp(s1 - t1[0]) / z
        return 0

    lax.fori_loop(0, P_HEADS, head, 0)


def _peer_route(h1, wq, k1, k2):
    t, d = h1.shape
    tm = min(256, t)
    out = pl.BlockSpec((P_HEADS, N_KEYS, tm), lambda i: (0, 0, i))
    return pl.pallas_call(
        _peer_route_body,
        grid=(t // tm,),
        in_specs=[pl.BlockSpec((tm, d), lambda i: (i, 0)), _full(wq.shape), _full(k1.shape), _full(k2.shape)],
        out_specs=[out] * 4,
        out_shape=[jax.ShapeDtypeStruct((P_HEADS, N_KEYS, t), F32)] * 4,
        scratch_shapes=[pltpu.VMEM((P_HEADS, tm, P_DQ), BF16)],
        compiler_params=_cparams("parallel"),
        name="peer_route",
    )(h1, wq, k1, k2)


def _peer_main_body(h1_ref, u_ref, vt_ref, s2_ref, e2_ref, th_ref, e1_ref, g_ref, b_ref, out_ref,
                    xt_ref, acc_ref, gw_ref):
    e = pl.program_id(1)
    tm = h1_ref.shape[0]
    nsl = PEER_TE // N_KEYS

    @pl.when(e == 0)
    def _init():
        xt_ref[...] = jnp.transpose(h1_ref[...]).astype(BF16)
        acc_ref[...] = jnp.zeros_like(acc_ref)

    at = _dot(u_ref[...], xt_ref[...])
    i0 = pl.multiple_of(e * nsl, nsl)
    for ii in range(nsl):
        def head(h, w):
            th = th_ref[h, pl.ds(i0, nsl), :][ii:ii + 1, :]
            e1 = e1_ref[h, pl.ds(i0, nsl), :][ii:ii + 1, :]
            return w + jnp.where(s2_ref[h] >= th, e2_ref[h], 0.0) * e1
        w = lax.fori_loop(0, P_HEADS, head, jnp.zeros((N_KEYS, tm), F32))
        a = at[ii * N_KEYS:(ii + 1) * N_KEYS, :]
        gw_ref[ii * N_KEYS:(ii + 1) * N_KEYS, :] = (jax.nn.gelu(a) * w).astype(BF16)
    acc_ref[...] += _dot(vt_ref[...], gw_ref[...])

    @pl.when(e == pl.num_programs(1) - 1)
    def _finish():
        y = DN_ALPHA * h1_ref[...] + jnp.transpose(acc_ref[...])
        out_ref[...] = _layer_norm(y, g_ref[...], b_ref[...])


def _peer_main(h1, u, vt, s2, e2, th, e1, g, b):
    t, d = h1.shape
    tm = min(512, t)
    route = pl.BlockSpec((P_HEADS, N_KEYS, tm), lambda i, e: (0, 0, i))
    return pl.pallas_call(
        _peer_main_body,
        grid=(t // tm, N_EXPERTS // PEER_TE),
        in_specs=[pl.BlockSpec((tm, d), lambda i, e: (i, 0)),
                  pl.BlockSpec((PEER_TE, d), lambda i, e: (e, 0)),
                  pl.BlockSpec((d, PEER_TE), lambda i, e: (0, e)),
                  route, route, route, route, _full(g.shape), _full(b.shape)],
        out_specs=pl.BlockSpec((tm, d), lambda i, e: (i, 0)),
        out_shape=jax.ShapeDtypeStruct((t, d), F32),
        scratch_shapes=[pltpu.VMEM((d, tm), BF16), pltpu.VMEM((d, tm), F32), pltpu.VMEM((PEER_TE, tm), BF16)],
        compiler_params=_cparams("parallel", "arbitrary"),
        name="peer_main",
    )(h1, u, vt, s2, e2, th, e1, g, b)


def _ple_body(h_ref, p_ref, gw_ref, gb_ref, pw_ref, out_ref):
    h = h_ref[...]
    gate = jax.nn.sigmoid(_dot(h.astype(BF16), gw_ref[...]) + gb_ref[...])
    out_ref[...] = h + gate * _dot(p_ref[...].astype(BF16), pw_ref[...])


def _ple(h, p, gw, gb, pw):
    t, d = h.shape
    tm = min(512, t)
    return pl.pallas_call(
        _ple_body,
        grid=(t // tm,),
        in_specs=[pl.BlockSpec((tm, d), lambda i: (i, 0)), pl.BlockSpec((tm, PLE_DIM), lambda i: (i, 0)),
                  _full(gw.shape), _full(gb.shape), _full(pw.shape)],
        out_specs=pl.BlockSpec((tm, d), lambda i: (i, 0)),
        out_shape=jax.ShapeDtypeStruct((t, d), F32),
        compiler_params=_cparams("parallel"),
        name="ple",
    )(h, p, gw, gb, pw)


def _t5_bucket(rel):
    nb = N_BUCKETS // 2
    max_exact = nb // 2
    n = jnp.abs(rel)
    large = max_exact + (jnp.log(jnp.maximum(n, 1).astype(F32) / max_exact)
                         / math.log(MAX_DISTANCE / max_exact) * (nb - max_exact)).astype(jnp.int32)
    large = jnp.minimum(large, nb - 1)
    return jnp.where(rel > 0, nb, 0) + jnp.where(n < max_exact, n, large)


def _bias_tables(tab, heads, blk):
    r = jnp.arange(blk, dtype=jnp.int32)[:, None]
    c = jnp.arange(blk, dtype=jnp.int32)[None, :]
    rel = jnp.stack([c - r, c - r - blk])
    tiles = tab[_t5_bucket(rel)][..., jnp.asarray(heads)]
    tiles = jnp.transpose(tiles, (3, 0, 1, 2)).reshape(len(heads) // 2, 2, 2, blk, blk)
    far = tab[_t5_bucket(jnp.int32(-(blk + 1)))][jnp.asarray(heads)]
    return tiles.astype(F32), far.astype(F32)


def _swap_half(w):
    half = w.shape[-1] // 2
    return jnp.concatenate([-w[..., half:], w[..., :half]], axis=-1)


def _even_weights(w_in, w_uq, w_ukv):
    c_q, c_kv, k_r = w_in[:, 0:512], w_in[:, 512:768], w_in[:, 768:800]
    rest = w_in[:, 800:2336]
    w_in2 = jnp.concatenate([c_q, c_kv, rest, jnp.tile(k_r, (1, 4)), jnp.tile(_swap_half(k_r), (1, 4))], axis=1)
    uq = w_uq.reshape(A_Q_RANK, H_A, A_NOPE + A_ROPE)
    rope = uq[:, :, A_NOPE:]
    w_uq2 = jnp.concatenate([uq[:, :, :A_NOPE].reshape(A_Q_RANK, -1), rope.reshape(A_Q_RANK, -1),
                             _swap_half(rope).reshape(A_Q_RANK, -1)], axis=1)
    ukv = w_ukv.reshape(A_KV_RANK, H_A, A_NOPE + A_V)
    w_ukv2 = jnp.concatenate([ukv[:, :, :A_NOPE].reshape(A_KV_RANK, -1),
                              ukv[:, :, A_NOPE:].reshape(A_KV_RANK, -1)], axis=1)
    return w_in2.astype(BF16), w_uq2.astype(BF16), w_ukv2.astype(BF16)


_ODD_HEADS = [(2 * pp + a) * C_GROUP + g for pp in range(KV_C // 2) for g in range(C_GROUP) for a in range(2)]


def _odd_weights(w_in, w_o):
    d = w_in.shape[0]
    heads = jnp.asarray(_ODD_HEADS)
    wq = w_in[:, 0:1024].reshape(d, H_C, C_DH)[:, heads].reshape(d, -1)
    wk, wv = w_in[:, 1024:1280], w_in[:, 1280:1536]
    wqi = w_in[:, 1536:2560].reshape(d, H_IDX, IDX_DH)
    wqi_s = jnp.concatenate([_swap_half(wqi[:, :, :IDX_ROPE]), jnp.zeros_like(wqi[:, :, IDX_ROPE:])], axis=-1)
    wki = w_in[:, 2560:2624]
    wki_s = jnp.concatenate([_swap_half(wki[:, :IDX_ROPE]), jnp.zeros_like(wki[:, IDX_ROPE:])], axis=-1)
    wwi = jnp.pad(w_in[:, 2624:2640], ((0, 0), (0, LANES - H_IDX)))
    w2 = jnp.concatenate([wq, wk, wv, wqi.reshape(d, -1), wqi_s.reshape(d, -1),
                          jnp.tile(wki, (1, 2)), jnp.tile(wki_s, (1, 2)), wwi], axis=1)
    w_o2 = w_o.reshape(H_C, C_DH, -1)[heads].reshape(H_C * C_DH, -1)
    return w2.astype(BF16), w_o2.astype(BF16)


def _rope_tables(positions):
    half = A_ROPE // 2
    freqs = ROPE_THETA ** (-jnp.arange(half, dtype=F32) / half)
    ang = positions.astype(F32)[..., None] * freqs
    cos, sin = jnp.cos(ang), jnp.sin(ang)
    cos32 = jnp.concatenate([cos, cos], axis=-1)
    sin32 = jnp.concatenate([sin, sin], axis=-1)
    ones, zeros = jnp.ones_like(cos32), jnp.zeros_like(sin32)
    return (jnp.tile(cos32, (1, 1, 4)), jnp.tile(sin32, (1, 1, 4)),
            jnp.tile(jnp.concatenate([cos32, ones], axis=-1), (1, 1, 2)),
            jnp.tile(jnp.concatenate([sin32, zeros], axis=-1), (1, 1, 2)))


def kernel(x, p, positions, rel_bias, ev_w_in, ev_w_uq, ev_w_ukv, ev_q_norm, ev_kv_norm, ev_lam_q1, ev_lam_k1, ev_lam_q2, ev_lam_k2, ev_subln, ev_w_o, od_w_in, od_w_o, ln1_g, ln1_b, ln2_g, ln2_b, peer_w_q, peer_k1, peer_k2, peer_u, peer_v, ple_w, ple_gate_w, ple_gate_b):
    b, s, d = x.shape
    t = b * s
    assert s % ATT_BLK == 0 and d == D_MODEL
    n_sel = min(TOPK_MAX, s // 4)
    cos_f, sin_f, cos_p, sin_p = _rope_tables(positions)
    bias_b, bfar_b = _bias_tables(rel_bias[:, :H_B], list(range(H_B)), ATT_BLK)
    bias_c, bfar_c = _bias_tables(rel_bias[:, H_B:], _ODD_HEADS, ATT_BLK)
    row = lambda a: a.reshape(1, -1).astype(F32)

    h = x
    for i in range(DEPTH):
        j = i // 2
        if i % 2 == 0:
            lam_init = 0.8 - 0.6 * math.exp(-0.3 * i)
            lam = (jnp.exp(jnp.sum(ev_lam_q1[j] * ev_lam_k1[j], dtype=F32))
                   - jnp.exp(jnp.sum(ev_lam_q2[j] * ev_lam_k2[j], dtype=F32)) + lam_init).reshape(1)
            w_in2, w_uq2, w_ukv2 = _even_weights(ev_w_in[j], ev_w_uq[j], ev_w_ukv[j])
            qn, qr, kn, kr, va, qd, kd, vd = _even_proj(h, cos_f, sin_f, w_in2, w_uq2, w_ukv2,
                                                        row(ev_q_norm[j]), row(ev_kv_norm[j]))
            sg = row(jnp.tile(ev_subln[j], 2))
            oa, od = _even_attn(lam, bfar_b, qn, qr, kn, kr, va, qd, kd, vd, bias_b, sg, lam_init)
            w_o = ev_w_o[j].astype(BF16)
            os_ = [oa.reshape(t, -1), od.reshape(t, -1)]
            ws = [w_o[:H_A * A_V], w_o[H_A * A_V:]]
        else:
            w_in2, w_o2 = _odd_weights(od_w_in[j], od_w_o[j])
            q, k, v, qi, ki, wi = _odd_proj(h, cos_p, sin_p, w_in2)
            o = _dsa_attn(bfar_c, qi, ki, wi, q, k, v, bias_c, n_sel)
            os_, ws = [o.reshape(t, -1)], [w_o2]
        h1 = _oproj_ln(os_, ws, h.reshape(t, d), row(ln1_g[i]), row(ln1_b[i]))
        s2, e2, th, e1 = _peer_route(h1, peer_w_q[i].astype(BF16), peer_k1[i].astype(BF16),
                                     peer_k2[i].astype(BF16))
        h2 = _peer_main(h1, peer_u[i].astype(BF16), jnp.transpose(peer_v[i]).astype(BF16),
                        s2, e2, th, e1, row(ln2_g[i]), row(ln2_b[i]))
        h = _ple(h2, p[i].reshape(t, PLE_DIM), ple_gate_w[i].astype(BF16), row(ple_gate_b[i]),
                 ple_w[i].astype(BF16)).reshape(b, s, d)
    return h
```

```python
import functools
import math

import numpy as np
import jax
import jax.numpy as jnp
from jax import lax
from jax.experimental import pallas as pl
from jax.experimental.pallas import tpu as pltpu

D_MODEL = 1024
DEPTH = 4
CHUNK = 64
PLE_DIM = 256
N_BUCKETS = 32
MAX_DISTANCE = 256
ROPE_THETA = 10000.0
LN_EPS = 1e-5
RMS_EPS = 1e-6
H_A, A_NOPE, A_ROPE, A_V, A_Q_RANK, A_KV_RANK = 8, 64, 32, 64, 512, 256
A_SCALE = (A_NOPE + A_ROPE) ** -0.5
H_B, B_DH = 8, 32
B_SCALE = B_DH ** -0.5
H_C, KV_C, C_DH = 16, 4, 64
C_GROUP = H_C // KV_C
C_SCALE = C_DH ** -0.5
H_IDX, IDX_DH, IDX_ROPE = 16, 64, 32
IDX_SCALE = IDX_DH ** -0.5
TOPK_MAX = 256
P_HEADS, N_KEYS, P_DQ, P_TOPK = 8, 128, 256, 16
P_DHALF = P_DQ // 2
N_EXPERTS = N_KEYS * N_KEYS
DN_ALPHA = (2 * DEPTH) ** 0.25

LANES = 128
ATT_BLK = 256
ATT_ROWS = 256
ATT_AHEAD = 2
DSA_PAIRS = 2
PEER_TE = 8 * N_KEYS
VMEM_LIMIT = 52 * 1024 * 1024

LOG2E = math.log2(math.e)

BF16 = jnp.bfloat16
F32 = jnp.float32
NEG_INF = float("-inf")
M_INIT = -1e30
INT_MIN = np.int32(-2 ** 31)


def _cparams(*sem):
    return pltpu.CompilerParams(dimension_semantics=sem, vmem_limit_bytes=VMEM_LIMIT)


def _full(shape):
    n = len(shape)
    return pl.BlockSpec(shape, lambda *_: (0,) * n)


def _smem():
    return pl.BlockSpec(memory_space=pltpu.SMEM)


def _rms(v, g):
    return v * lax.rsqrt(jnp.mean(v * v, axis=-1, keepdims=True) + RMS_EPS) * g


def _layer_norm(y, g, b):
    mu = jnp.mean(y, axis=-1, keepdims=True)
    yc = y - mu
    var = jnp.mean(yc * yc, axis=-1, keepdims=True)
    return yc * lax.rsqrt(var + LN_EPS) * g + b


def _dot(a, b):
    return jnp.dot(a, b, preferred_element_type=F32)


def _dot_nt(a, b):
    return lax.dot_general(a, b, (((1,), (1,)), ((), ())), preferred_element_type=F32)


def _even_proj_body(h_ref, cos_ref, sin_ref, w_in_ref, w_uq_ref, w_ukv_ref, qg_ref, kvg_ref,
                    qn_ref, qr_ref, kn_ref, kr_ref, va_ref, qd_ref, kd_ref, vd_ref):
    x = h_ref[0].astype(BF16)
    acc = _dot(x, w_in_ref[...])
    cos = cos_ref[0]
    sin = sin_ref[0]
    cq = _rms(acc[:, 0:512], qg_ref[...]).astype(BF16)
    qa = _dot(cq, w_uq_ref[...])
    qn_ref[0] = (qa[:, 0:512] * (A_SCALE * LOG2E)).astype(BF16)
    cos2 = jnp.concatenate([cos, cos], axis=1)
    sin2 = jnp.concatenate([sin, sin], axis=1)
    qr_ref[0] = ((qa[:, 512:768] * cos2 + qa[:, 768:1024] * sin2) * (A_SCALE * LOG2E)).astype(BF16)
    ckv = _rms(acc[:, 512:768], kvg_ref[...]).astype(BF16)
    kv = _dot(ckv, w_ukv_ref[...])
    ones_hi = (lax.broadcasted_iota(jnp.int32, (1, 1024), 1) % LANES >= LANES // 2).astype(F32)
    kn_ref[0] = kv[:, 0:512].astype(BF16)
    va_ref[0] = (kv[:, 512:1536] + ones_hi).astype(BF16)
    qd_ref[0] = (acc[:, 768:1280] * (B_SCALE * LOG2E)).astype(BF16)
    kd_ref[0] = acc[:, 1280:1792].astype(BF16)
    vd_ref[0] = (acc[:, 1792:2816] + ones_hi).astype(BF16)
    kr_ref[0] = (acc[:, 2816:2944] * cos + acc[:, 2944:3072] * sin).astype(BF16)


def _even_proj(h, cos, sin, w_in, w_uq, w_ukv, qg, kvg):
    b, s, d = h.shape
    tm = min(512, s)
    tok = lambda w: pl.BlockSpec((1, tm, w), lambda i, j: (i, j, 0))
    widths = (512, 256, 512, 128, 1024, 512, 512, 1024)
    return pl.pallas_call(
        _even_proj_body,
        grid=(b, s // tm),
        in_specs=[tok(d), tok(LANES), tok(LANES), _full(w_in.shape), _full(w_uq.shape),
                  _full(w_ukv.shape), _full(qg.shape), _full(kvg.shape)],
        out_specs=[tok(w) for w in widths],
        out_shape=[jax.ShapeDtypeStruct((b, s, w), BF16) for w in widths],
        compiler_params=_cparams("parallel", "parallel"),
        name="even_proj",
    )(h, cos, sin, w_in, w_uq, w_ukv, qg, kvg)


def _flash_update(m_ref, acc_ref, i, rows, s, v):
    m_prev = m_ref[i, rows, :]
    m_new = jnp.maximum(m_prev, jnp.max(s, axis=1, keepdims=True))
    alpha = jnp.exp2(m_prev - m_new)
    p = jnp.exp2(s - jnp.concatenate([m_new] * (s.shape[1] // LANES), axis=1))
    acc_ref[i, rows, :] = alpha * acc_ref[i, rows, :] + _dot(p.astype(BF16), v)
    m_ref[i, rows, :] = m_new


def _row_tiles(blk):
    return [slice(r, r + ATT_ROWS) for r in range(0, blk, ATT_ROWS)]


def _lookahead(tiles, produce, consume):
    pending = [produce(t) for t in tiles[:ATT_AHEAD]]
    for n, t in enumerate(tiles):
        if n + ATT_AHEAD < len(tiles):
            pending.append(produce(tiles[n + ATT_AHEAD]))
        consume(t, pending.pop(0))


def _normalized(acc, hh):
    r = pltpu.roll(acc, LANES // 2, 1)
    return acc / r if hh == 0 else r / acc


def _chunk_allowed(tq, tk):
    r = lax.broadcasted_iota(jnp.int32, (tq, tk), 0) // CHUNK
    c = lax.broadcasted_iota(jnp.int32, (tq, tk), 1) // CHUNK
    return c <= r


def _even_attn_body(lam_ref, bfar_ref, qn_ref, qr_ref, kn_ref, kr_ref, va_ref, qd_ref, kd_ref, vd_ref,
                    bias_ref, sg_ref, oa_ref, od_ref, m_ref, acc_ref, *, blk, lam_init):
    p = pl.program_id(1)
    qi = pl.program_id(2)
    lane = lax.broadcasted_iota(jnp.int32, (blk, LANES), 1)
    qn = qn_ref[0]
    qr = qr_ref[0]
    qd = qd_ref[0]
    zero = jnp.zeros_like(qn)
    streams = []
    for hh in range(2):
        lo = 64 * hh
        r0 = 32 * (2 * (p % 2) + hh)
        qn_m = jnp.where((lane >= lo) & (lane < lo + 64), qn, zero)
        qr_m = jnp.where((lane >= r0) & (lane < r0 + 32), qr, zero)
        streams.append((jnp.concatenate([qn_m, qr_m], axis=1), "a", hh))
        streams.append((jnp.where((lane >= lo) & (lane < lo + 32), qd, zero), "d", hh))
        streams.append((jnp.where((lane >= lo + 32) & (lane < lo + 64), qd, zero), "d", hh))
    allowed = _chunk_allowed(blk, blk)

    m_ref[...] = jnp.full(m_ref.shape, M_INIT, F32)
    acc_ref[...] = jnp.zeros(acc_ref.shape, F32)

    def block(j, near):
        ks = pl.multiple_of(j * blk, blk)
        ka = jnp.concatenate([kn_ref[0, pl.ds(ks, blk), :], kr_ref[0, pl.ds(ks, blk), :]], axis=1)
        kd = kd_ref[0, pl.ds(ks, blk), :]
        def scores(tile):
            i, rows = tile
            q, kind, hh = streams[i]
            if kind == "a":
                s = _dot_nt(q[rows], ka)
            else:
                s = _dot_nt(q[rows], kd)
                s = s + (bfar_ref[2 * p + hh] if near is None else bias_ref[0, hh, near, rows, :])
            if near == 0:
                s = jnp.where(allowed[rows], s, NEG_INF)
            return s

        def update(tile, s):
            i, rows = tile
            _, kind, hh = streams[i]
            v_ref = va_ref if kind == "a" else vd_ref
            _flash_update(m_ref, acc_ref, i, rows, s, v_ref[0, pl.ds(ks, blk), hh * LANES:(hh + 1) * LANES])

        _lookahead([(i, rows) for i in range(len(streams)) for rows in _row_tiles(blk)], scores, update)

    def far(j, carry):
        block(j, None)
        return carry

    lax.fori_loop(0, jnp.maximum(qi - 1, 0), far, 0)

    @pl.when(qi >= 1)
    def _previous():
        block(qi - 1, 1)

    block(qi, 0)

    lam = lam_ref[0]
    first = lane < 64
    o = [_normalized(acc_ref[i], hh) for i, (_, _, hh) in enumerate(streams)]
    oa_ref[0] = jnp.where(first, o[0], o[3]).astype(BF16)
    od = jnp.where(first, o[1] - lam * o[2], o[4] - lam * o[5])
    sq = od * od
    ms0 = jnp.sum(jnp.where(first, sq, 0.0), axis=1, keepdims=True)
    ms1 = jnp.sum(jnp.where(first, 0.0, sq), axis=1, keepdims=True)
    ms = jnp.where(first, ms0, ms1) * (1.0 / (2 * B_DH))
    od_ref[0] = (od * lax.rsqrt(ms + RMS_EPS) * sg_ref[...] * (1.0 - lam_init)).astype(BF16)


def _even_attn(lam, bfar, qn, qr, kn, kr, va, qd, kd, vd, bias, sg, lam_init):
    b, s, _ = qn.shape
    blk = ATT_BLK
    qblk = lambda f: pl.BlockSpec((1, blk, LANES), f)
    kv = lambda f: pl.BlockSpec((1, s, LANES), f)
    vv = lambda f: pl.BlockSpec((1, s, 2 * LANES), f)
    body = functools.partial(_even_attn_body, blk=blk, lam_init=lam_init)
    return pl.pallas_call(
        body,
        grid=(b, H_A // 2, s // blk),
        in_specs=[_smem(), _smem(),
                  qblk(lambda i, p, q: (i, q, p)), qblk(lambda i, p, q: (i, q, p // 2)),
                  kv(lambda i, p, q: (i, 0, p)), kv(lambda i, p, q: (i, 0, 0)), vv(lambda i, p, q: (i, 0, p)),
                  qblk(lambda i, p, q: (i, q, p)), kv(lambda i, p, q: (i, 0, p)), vv(lambda i, p, q: (i, 0, p)),
                  pl.BlockSpec((1, 2, 2, blk, blk), lambda i, p, q: (p, 0, 0, 0, 0)),
                  _full(sg.shape)],
        out_specs=[qblk(lambda i, p, q: (i, q, p)), qblk(lambda i, p, q: (i, q, p))],
        out_shape=[jax.ShapeDtypeStruct((b, s, 512), BF16)] * 2,
        scratch_shapes=[pltpu.VMEM((6, blk, LANES), F32), pltpu.VMEM((6, blk, LANES), F32)],
        compiler_params=_cparams("parallel", "parallel", "arbitrary"),
        name="even_attn",
    )(lam, bfar, qn, qr, kn, kr, va, qd, kd, vd, bias, sg)


def _oproj_ln_body(*refs, n_in):
    o_refs = refs[:n_in]
    w_refs = refs[n_in:2 * n_in]
    h_ref, g_ref, b_ref, out_ref = refs[2 * n_in:]
    mix = _dot(o_refs[0][...], w_refs[0][...])
    for o_ref, w_ref in zip(o_refs[1:], w_refs[1:]):
        mix = mix + _dot(o_ref[...], w_ref[...])
    out_ref[...] = _layer_norm(DN_ALPHA * h_ref[...] + mix, g_ref[...], b_ref[...])


def _oproj_ln(os_, ws, h, g, b):
    t, d = h.shape
    tm = min(512, t)
    tok = lambda w: pl.BlockSpec((tm, w), lambda i: (i, 0))
    return pl.pallas_call(
        functools.partial(_oproj_ln_body, n_in=len(os_)),
        grid=(t // tm,),
        in_specs=[tok(o.shape[1]) for o in os_] + [_full(w.shape) for w in ws]
                 + [tok(d), _full(g.shape), _full(b.shape)],
        out_specs=tok(d),
        out_shape=jax.ShapeDtypeStruct((t, d), F32),
        compiler_params=_cparams("parallel"),
        name="oproj_ln",
    )(*os_, *ws, h, g, b)


def _odd_proj_body(h_ref, cos_ref, sin_ref, w_ref, q_ref, k_ref, v_ref, qi_ref, ki_ref, wi_ref):
    x = h_ref[0].astype(BF16)
    acc = _dot(x, w_ref[...])
    cos = cos_ref[0]
    sin = sin_ref[0]
    q_ref[0] = (acc[:, 0:1024] * (C_SCALE * LOG2E)).astype(BF16)
    k_ref[0] = acc[:, 1024:1280].astype(BF16)
    ones_hi = (lax.broadcasted_iota(jnp.int32, (1, 512), 1) % LANES >= LANES // 2).astype(F32)
    v_ref[0] = (acc[:, 1280:1792] + ones_hi).astype(BF16)
    cos8 = jnp.concatenate([cos] * 8, axis=1)
    sin8 = jnp.concatenate([sin] * 8, axis=1)
    qi_ref[0] = ((acc[:, 1792:2816] * cos8 + acc[:, 2816:3840] * sin8) * IDX_SCALE).astype(BF16)
    ki_ref[0] = (acc[:, 3840:3968] * cos + acc[:, 3968:4096] * sin).astype(BF16)
    wi_ref[0] = acc[:, 4096:4224] * (H_IDX ** -0.5)


def _odd_proj(h, cos, sin, w):
    b, s, d = h.shape
    tm = min(256, s)
    tok = lambda wd: pl.BlockSpec((1, tm, wd), lambda i, j: (i, j, 0))
    widths = (1024, 256, 512, 1024, 128, 128)
    dts = (BF16, BF16, BF16, BF16, BF16, F32)
    return pl.pallas_call(
        _odd_proj_body,
        grid=(b, s // tm),
        in_specs=[tok(d), tok(LANES), tok(LANES), _full(w.shape)],
        out_specs=[tok(wd) for wd in widths],
        out_shape=[jax.ShapeDtypeStruct((b, s, wd), dt) for wd, dt in zip(widths, dts)],
        compiler_params=_cparams("parallel", "parallel"),
        name="odd_proj",
    )(h, cos, sin, w)


def _dsa_body(bfar_ref, qi_ref, ki_ref, wi_ref, q_ref, k_ref, v0_ref, v1_ref, bias_ref, o_ref,
              key_ref, madd_ref, m_ref, acc_ref, *, blk, n_sel):
    qb = pl.program_id(1)
    hb = pl.program_id(2)
    lane = lax.broadcasted_iota(jnp.int32, (blk, LANES), 1)
    first = lane < 64

    @pl.when(hb == 0)
    def _select():
        wt = jnp.transpose(wi_ref[0])
        kc = lax.broadcasted_iota(jnp.int32, (blk, blk), 0) // CHUNK
        qc = lax.broadcasted_iota(jnp.int32, (blk, blk), 1) // CHUNK
        chunk_mask = jnp.where(kc <= qc, 0.0, NEG_INF)
        qis = []
        for hp in range(H_IDX // 2):
            qp = qi_ref[0, :, hp * LANES:(hp + 1) * LANES]
            qis.append(jnp.where(first, qp, jnp.zeros_like(qp)))
            qis.append(jnp.where(first, jnp.zeros_like(qp), qp))

        def score(j, _):
            ks = pl.multiple_of(j * blk, blk)
            for rows in (slice(0, blk // 2), slice(blk // 2, blk)):
                kib = ki_ref[0, pl.ds(ks + rows.start, blk // 2), :]
                sc = jnp.zeros((blk // 2, blk), F32)
                for h in range(H_IDX):
                    sc = sc + jnp.maximum(_dot_nt(kib, qis[h]), 0.0) * wt[h:h + 1, :]
                sc = sc + jnp.where(j < qb, 0.0, chunk_mask[rows])
                bits = pltpu.bitcast(sc, jnp.int32)
                key_ref[j, rows, :] = jnp.where(bits < 0, bits ^ np.int32(0x7FFFFFFF), bits)
            return 0

        lax.fori_loop(0, qb + 1, score, 0)

        def count_ge(cand):
            def body(j, part):
                sel = jnp.where(key_ref[j] >= cand, 1.0, 0.0)
                return part + jnp.sum(sel.reshape(blk // 8, 8, blk), axis=0)
            part = lax.fori_loop(0, qb + 1, body, jnp.zeros((8, blk), F32))
            return jnp.sum(part, axis=0, keepdims=True)

        n = float(n_sel)
        zero_i = jnp.zeros((1, blk), jnp.int32)
        ans = jnp.where(count_ge(zero_i) >= n, zero_i, zero_i + INT_MIN)

        def bit_step(t, ans):
            cand = ans | lax.shift_left(jnp.int32(1), 30 - t)
            return jnp.where(count_ge(cand) >= n, cand, ans)

        thr = lax.fori_loop(0, 31, bit_step, ans)

        def make_mask(j, _):
            sel = jnp.where(key_ref[j] >= thr, 0.0, NEG_INF) + jnp.where(j < qb, 0.0, chunk_mask)
            madd_ref[j] = jnp.transpose(sel)
            return 0

        lax.fori_loop(0, qb + 1, make_mask, 0)

    heads = []
    for g in range(DSA_PAIRS):
        q = q_ref[0, :, g * LANES:(g + 1) * LANES]
        zq = jnp.zeros_like(q)
        heads.append((jnp.where(first, q, zq), g, 0))
        heads.append((jnp.where(first, zq, q), g, 1))
    v_refs = (v0_ref, v1_ref)
    m_ref[...] = jnp.full(m_ref.shape, M_INIT, F32)
    acc_ref[...] = jnp.zeros(acc_ref.shape, F32)

    def block(j, near):
        ks = pl.multiple_of(j * blk, blk)
        kb = k_ref[0, pl.ds(ks, blk), :]

        def scores(tile):
            i, rows = tile
            q, g, hh = heads[i]
            s = _dot_nt(q[rows], kb) + madd_ref[j, rows, :]
            if near is None:
                return s + bfar_ref[2 * DSA_PAIRS * hb + 2 * g + hh]
            return s + bias_ref[0, g, hh, near, rows, :]

        def update(tile, s):
            i, rows = tile
            _flash_update(m_ref, acc_ref, i, rows, s, v_refs[heads[i][2]][0, pl.ds(ks, blk), :])

        _lookahead([(i, rows) for i in range(len(heads)) for rows in _row_tiles(blk)], scores, update)

    def far(j, carry):
        block(j, None)
        return carry

    lax.fori_loop(0, jnp.maximum(qb - 1, 0), far, 0)

    @pl.when(qb >= 1)
    def _previous():
        block(qb - 1, 1)

    block(qb, 0)
    for g in range(DSA_PAIRS):
        o = jnp.where(first, _normalized(acc_ref[2 * g], 0), _normalized(acc_ref[2 * g + 1], 1))
        o_ref[0, :, g * LANES:(g + 1) * LANES] = o.astype(BF16)


def _dsa_attn(bfar, qi, ki, wi, q, k, v, bias, n_sel):
    b, s, _ = q.shape
    blk = ATT_BLK
    nk = s // blk
    nhb = H_C // (2 * DSA_PAIRS)
    width = DSA_PAIRS * LANES
    kvp = lambda hb: hb * DSA_PAIRS // C_GROUP
    bias = bias.reshape(nhb, DSA_PAIRS, 2, 2, blk, blk)
    body = functools.partial(_dsa_body, blk=blk, n_sel=n_sel)
    return pl.pallas_call(
        body,
        grid=(b, nk, nhb),
        in_specs=[_smem(),
                  pl.BlockSpec((1, blk, 1024), lambda i, qb, hb: (i, qb, 0)),
                  pl.BlockSpec((1, s, LANES), lambda i, qb, hb: (i, 0, 0)),
                  pl.BlockSpec((1, blk, LANES), lambda i, qb, hb: (i, qb, 0)),
                  pl.BlockSpec((1, blk, width), lambda i, qb, hb: (i, qb, hb)),
                  pl.BlockSpec((1, s, LANES), lambda i, qb, hb: (i, 0, kvp(hb))),
                  pl.BlockSpec((1, s, LANES), lambda i, qb, hb: (i, 0, 2 * kvp(hb))),
                  pl.BlockSpec((1, s, LANES), lambda i, qb, hb: (i, 0, 2 * kvp(hb) + 1)),
                  pl.BlockSpec((1, DSA_PAIRS, 2, 2, blk, blk), lambda i, qb, hb: (hb, 0, 0, 0, 0, 0))],
        out_specs=pl.BlockSpec((1, blk, width), lambda i, qb, hb: (i, qb, hb)),
        out_shape=jax.ShapeDtypeStruct((b, s, H_C * C_DH), BF16),
        scratch_shapes=[pltpu.VMEM((nk, blk, blk), jnp.int32),
                        pltpu.VMEM((nk, blk, blk), F32),
                        pltpu.VMEM((2 * DSA_PAIRS, blk, LANES), F32),
                        pltpu.VMEM((2 * DSA_PAIRS, blk, LANES), F32)],
        compiler_params=_cparams("parallel", "arbitrary", "arbitrary"),
        name="dsa_attn",
    )(bfar, qi, ki, wi, q, k, v, v, bias)


def _top_rows(cur, n):
    vals = []
    for _ in range(n):
        m = jnp.max(cur, axis=0, keepdims=True)
        vals.append(m)
        cur = jnp.where(cur == m, NEG_INF, cur)
    return vals


def _peer_route_body(h_ref, wq_ref, k1_ref, k2_ref, s2_ref, e2_ref, th_ref, e1_ref, q_scr):
    tm = h_ref.shape[0]
    q = _dot(h_ref[...].astype(BF16), wq_ref[...]).astype(BF16)
    for h in range(P_HEADS):
        q_scr[h] = q[:, h * P_DQ:(h + 1) * P_DQ]
    row16 = lax.broadcasted_iota(jnp.int32, (P_TOPK, tm), 0)

    def head(h, _):
        qh = q_scr[h]
        s1 = _dot_nt(k1_ref[...], qh[:, 0:P_DHALF])
        s2 = _dot_nt(k2_ref[...], qh[:, P_DHALF:P_DQ])
        t1 = _top_rows(s1, P_TOPK + 1)
        t2 = _top_rows(s2, P_TOPK + 1)
        t1m = jnp.zeros((P_TOPK, tm), F32)
        t2m = jnp.zeros((P_TOPK, tm), F32)
        for r in range(P_TOPK):
            t1m = jnp.where(row16 == r, t1[r], t1m)
            t2m = jnp.where(row16 == r, t2[r], t2m)
        cand = jnp.concatenate([t1[r] + t2m for r in range(4)]
                               + [t1[r] + t2m[0:8] for r in range(4, 8)]
                               + [t1m[8:16] + t2[0]], axis=0)
        c = _top_rows(cand, P_TOPK + 1)
        c17 = jnp.maximum(c[P_TOPK], jnp.maximum(t1[P_TOPK] + t2[0], t1[0] + t2[P_TOPK]))
        tau = 0.5 * (c[P_TOPK - 1] + c17)
        z = jnp.sum(jnp.where(cand >= tau, jnp.exp(cand - (t1[0] + t2[0])), 0.0), axis=0, keepdims=True)
        s2_ref[h] = s2
        e2_ref[h] = jnp.exp(s2 - t2[0])
        th_ref[h] = tau - s1
        e1_ref[h] = jnp.exp(s1 - t1[0]) / z
        return 0

    lax.fori_loop(0, P_HEADS, head, 0)


def _peer_route(h1, wq, k1, k2):
    t, d = h1.shape
    tm = min(256, t)
    out = pl.BlockSpec((P_HEADS, N_KEYS, tm), lambda i: (0, 0, i))
    return pl.pallas_call(
        _peer_route_body,
        grid=(t // tm,),
        in_specs=[pl.BlockSpec((tm, d), lambda i: (i, 0)), _full(wq.shape), _full(k1.shape), _full(k2.shape)],
        out_specs=[out] * 4,
        out_shape=[jax.ShapeDtypeStruct((P_HEADS, N_KEYS, t), F32)] * 4,
        scratch_shapes=[pltpu.VMEM((P_HEADS, tm, P_DQ), BF16)],
        compiler_params=_cparams("parallel"),
        name="peer_route",
    )(h1, wq, k1, k2)


def _peer_main_body(h1_ref, u_ref, vt_ref, s2_ref, e2_ref, th_ref, e1_ref, g_ref, b_ref, out_ref,
                    xt_ref, acc_ref, at_ref, gw_ref):
    e = pl.program_id(1)
    tm = h1_ref.shape[0]
    nsl = PEER_TE // N_KEYS

    @pl.when(e == 0)
    def _init():
        xt_ref[...] = jnp.transpose(h1_ref[...]).astype(BF16)
        acc_ref[...] = jnp.zeros_like(acc_ref)

    i0 = pl.multiple_of(e * nsl, nsl)
    quarter = 2 * N_KEYS

    def up_proj(q):
        rows = slice(q * quarter, (q + 1) * quarter)
        at_ref[rows, :] = _dot(u_ref[rows, :], xt_ref[...])

    def gate(q):
        slabs = (2 * q, 2 * q + 1)
        for c in range(tm // LANES):
            ls = slice(c * LANES, (c + 1) * LANES)
            ws = [jnp.zeros((N_KEYS, LANES), F32) for _ in slabs]
            for h in range(P_HEADS):
                s2t = s2_ref[h, :, ls]
                e2t = e2_ref[h, :, ls]
                thg = th_ref[h, pl.ds(i0, nsl), ls]
                e1g = e1_ref[h, pl.ds(i0, nsl), ls]
                for k, ii in enumerate(slabs):
                    ws[k] = ws[k] + jnp.where(s2t >= thg[ii:ii + 1, :], e2t, 0.0) * e1g[ii:ii + 1, :]
            for k, ii in enumerate(slabs):
                rows = slice(ii * N_KEYS, (ii + 1) * N_KEYS)
                gw_ref[rows, ls] = (jax.nn.gelu(at_ref[rows, ls]) * ws[k]).astype(BF16)

    half = PEER_TE // 2
    partial = []
    up_proj(0)
    for q in range(4):
        if q + 1 < 4:
            up_proj(q + 1)
        gate(q)
        if q % 2 == 1:
            ks = slice((q // 2) * half, (q // 2 + 1) * half)
            partial.append(_dot(vt_ref[:, ks], gw_ref[ks, :]))
    acc_ref[...] += partial[0] + partial[1]

    @pl.when(e == pl.num_programs(1) - 1)
    def _finish():
        y = DN_ALPHA * h1_ref[...] + jnp.transpose(acc_ref[...])
        out_ref[...] = _layer_norm(y, g_ref[...], b_ref[...])


def _peer_main(h1, u, vt, s2, e2, th, e1, g, b):
    t, d = h1.shape
    tm = min(512, t)
    route = pl.BlockSpec((P_HEADS, N_KEYS, tm), lambda i, e: (0, 0, i))
    return pl.pallas_call(
        _peer_main_body,
        grid=(t // tm, N_EXPERTS // PEER_TE),
        in_specs=[pl.BlockSpec((tm, d), lambda i, e: (i, 0)),
                  pl.BlockSpec((PEER_TE, d), lambda i, e: (e, 0)),
                  pl.BlockSpec((d, PEER_TE), lambda i, e: (0, e)),
                  route, route, route, route, _full(g.shape), _full(b.shape)],
        out_specs=pl.BlockSpec((tm, d), lambda i, e: (i, 0)),
        out_shape=jax.ShapeDtypeStruct((t, d), F32),
        scratch_shapes=[pltpu.VMEM((d, tm), BF16), pltpu.VMEM((d, tm), F32),
                        pltpu.VMEM((PEER_TE, tm), F32), pltpu.VMEM((PEER_TE, tm), BF16)],
        compiler_params=_cparams("parallel", "arbitrary"),
        name="peer_main",
    )(h1, u, vt, s2, e2, th, e1, g, b)


def _ple_body(h_ref, p_ref, gw_ref, gb_ref, pw_ref, out_ref):
    h = h_ref[...]
    gate = jax.nn.sigmoid(_dot(h.astype(BF16), gw_ref[...]) + gb_ref[...])
    out_ref[...] = h + gate * _dot(p_ref[...].astype(BF16), pw_ref[...])


def _ple(h, p, gw, gb, pw):
    t, d = h.shape
    tm = min(512, t)
    return pl.pallas_call(
        _ple_body,
        grid=(t // tm,),
        in_specs=[pl.BlockSpec((tm, d), lambda i: (i, 0)), pl.BlockSpec((tm, PLE_DIM), lambda i: (i, 0)),
                  _full(gw.shape), _full(gb.shape), _full(pw.shape)],
        out_specs=pl.BlockSpec((tm, d), lambda i: (i, 0)),
        out_shape=jax.ShapeDtypeStruct((t, d), F32),
        compiler_params=_cparams("parallel"),
        name="ple",
    )(h, p, gw, gb, pw)


def _t5_bucket(rel):
    nb = N_BUCKETS // 2
    max_exact = nb // 2
    n = jnp.abs(rel)
    large = max_exact + (jnp.log(jnp.maximum(n, 1).astype(F32) / max_exact)
                         / math.log(MAX_DISTANCE / max_exact) * (nb - max_exact)).astype(jnp.int32)
    large = jnp.minimum(large, nb - 1)
    return jnp.where(rel > 0, nb, 0) + jnp.where(n < max_exact, n, large)


def _bias_tables(tab, heads, blk):
    r = jnp.arange(blk, dtype=jnp.int32)[:, None]
    c = jnp.arange(blk, dtype=jnp.int32)[None, :]
    rel = jnp.stack([c - r, c - r - blk])
    tiles = tab[_t5_bucket(rel)][..., jnp.asarray(heads)]
    tiles = jnp.transpose(tiles, (3, 0, 1, 2)).reshape(len(heads) // 2, 2, 2, blk, blk)
    far = tab[_t5_bucket(jnp.int32(-(blk + 1)))][jnp.asarray(heads)]
    return tiles.astype(F32) * LOG2E, far.astype(F32) * LOG2E


def _pad_heads(w, n_heads):
    rows = w.shape[0]
    w = w.reshape(rows, n_heads, LANES // 2)
    return jnp.concatenate([w, jnp.zeros_like(w)], axis=-1).reshape(rows, n_heads * LANES)


def _swap_half(w):
    half = w.shape[-1] // 2
    return jnp.concatenate([-w[..., half:], w[..., :half]], axis=-1)


def _even_weights(w_in, w_uq, w_ukv):
    c_q, c_kv, k_r = w_in[:, 0:512], w_in[:, 512:768], w_in[:, 768:800]
    qk_d, v_d = w_in[:, 800:1824], w_in[:, 1824:2336]
    w_in2 = jnp.concatenate([c_q, c_kv, qk_d, _pad_heads(v_d, H_B), jnp.tile(k_r, (1, 4)),
                             jnp.tile(_swap_half(k_r), (1, 4))], axis=1)
    uq = w_uq.reshape(A_Q_RANK, H_A, A_NOPE + A_ROPE)
    rope = uq[:, :, A_NOPE:]
    w_uq2 = jnp.concatenate([uq[:, :, :A_NOPE].reshape(A_Q_RANK, -1), rope.reshape(A_Q_RANK, -1),
                             _swap_half(rope).reshape(A_Q_RANK, -1)], axis=1)
    ukv = w_ukv.reshape(A_KV_RANK, H_A, A_NOPE + A_V)
    w_ukv2 = jnp.concatenate([ukv[:, :, :A_NOPE].reshape(A_KV_RANK, -1),
                              _pad_heads(ukv[:, :, A_NOPE:].reshape(A_KV_RANK, -1), H_A)], axis=1)
    return w_in2.astype(BF16), w_uq2.astype(BF16), w_ukv2.astype(BF16)


_ODD_HEADS = [(2 * pp + a) * C_GROUP + g for pp in range(KV_C // 2) for g in range(C_GROUP) for a in range(2)]


def _odd_weights(w_in, w_o):
    d = w_in.shape[0]
    heads = jnp.asarray(_ODD_HEADS)
    wq = w_in[:, 0:1024].reshape(d, H_C, C_DH)[:, heads].reshape(d, -1)
    wk, wv = w_in[:, 1024:1280], w_in[:, 1280:1536]
    wqi = w_in[:, 1536:2560].reshape(d, H_IDX, IDX_DH)
    wqi_s = jnp.concatenate([_swap_half(wqi[:, :, :IDX_ROPE]), jnp.zeros_like(wqi[:, :, IDX_ROPE:])], axis=-1)
    wki = w_in[:, 2560:2624]
    wki_s = jnp.concatenate([_swap_half(wki[:, :IDX_ROPE]), jnp.zeros_like(wki[:, IDX_ROPE:])], axis=-1)
    wwi = jnp.pad(w_in[:, 2624:2640], ((0, 0), (0, LANES - H_IDX)))
    w2 = jnp.concatenate([wq, wk, _pad_heads(wv, KV_C), wqi.reshape(d, -1), wqi_s.reshape(d, -1),
                          jnp.tile(wki, (1, 2)), jnp.tile(wki_s, (1, 2)), wwi], axis=1)
    w_o2 = w_o.reshape(H_C, C_DH, -1)[heads].reshape(H_C * C_DH, -1)
    return w2.astype(BF16), w_o2.astype(BF16)


def _rope_tables(positions):
    half = A_ROPE // 2
    freqs = ROPE_THETA ** (-jnp.arange(half, dtype=F32) / half)
    ang = positions.astype(F32)[..., None] * freqs
    cos, sin = jnp.cos(ang), jnp.sin(ang)
    cos32 = jnp.concatenate([cos, cos], axis=-1)
    sin32 = jnp.concatenate([sin, sin], axis=-1)
    ones, zeros = jnp.ones_like(cos32), jnp.zeros_like(sin32)
    return (jnp.tile(cos32, (1, 1, 4)), jnp.tile(sin32, (1, 1, 4)),
            jnp.tile(jnp.concatenate([cos32, ones], axis=-1), (1, 1, 2)),
            jnp.tile(jnp.concatenate([sin32, zeros], axis=-1), (1, 1, 2)))


def kernel(x, p, positions, rel_bias, ev_w_in, ev_w_uq, ev_w_ukv, ev_q_norm, ev_kv_norm, ev_lam_q1, ev_lam_k1, ev_lam_q2, ev_lam_k2, ev_subln, ev_w_o, od_w_in, od_w_o, ln1_g, ln1_b, ln2_g, ln2_b, peer_w_q, peer_k1, peer_k2, peer_u, peer_v, ple_w, ple_gate_w, ple_gate_b):
    b, s, d = x.shape
    t = b * s
    assert s % ATT_BLK == 0 and d == D_MODEL
    n_sel = min(TOPK_MAX, s // 4)
    cos_f, sin_f, cos_p, sin_p = _rope_tables(positions)
    bias_b, bfar_b = _bias_tables(rel_bias[:, :H_B], list(range(H_B)), ATT_BLK)
    bias_c, bfar_c = _bias_tables(rel_bias[:, H_B:], _ODD_HEADS, ATT_BLK)
    row = lambda a: a.reshape(1, -1).astype(F32)

    h = x
    for i in range(DEPTH):
        j = i // 2
        if i % 2 == 0:
            lam_init = 0.8 - 0.6 * math.exp(-0.3 * i)
            lam = (jnp.exp(jnp.sum(ev_lam_q1[j] * ev_lam_k1[j], dtype=F32))
                   - jnp.exp(jnp.sum(ev_lam_q2[j] * ev_lam_k2[j], dtype=F32)) + lam_init).reshape(1)
            w_in2, w_uq2, w_ukv2 = _even_weights(ev_w_in[j], ev_w_uq[j], ev_w_ukv[j])
            qn, qr, kn, kr, va, qd, kd, vd = _even_proj(h, cos_f, sin_f, w_in2, w_uq2, w_ukv2,
                                                        row(ev_q_norm[j]), row(ev_kv_norm[j]))
            sg = row(jnp.tile(ev_subln[j], 2))
            oa, od = _even_attn(lam, bfar_b, qn, qr, kn, kr, va, qd, kd, vd, bias_b, sg, lam_init)
            w_o = ev_w_o[j].astype(BF16)
            os_ = [oa.reshape(t, -1), od.reshape(t, -1)]
            ws = [w_o[:H_A * A_V], w_o[H_A * A_V:]]
        else:
            w_in2, w_o2 = _odd_weights(od_w_in[j], od_w_o[j])
            q, k, v, qi, ki, wi = _odd_proj(h, cos_p, sin_p, w_in2)
            o = _dsa_attn(bfar_c, qi, ki, wi, q, k, v, bias_c, n_sel)
            os_, ws = [o.reshape(t, -1)], [w_o2]
        h1 = _oproj_ln(os_, ws, h.reshape(t, d), row(ln1_g[i]), row(ln1_b[i]))
        s2, e2, th, e1 = _peer_route(h1, peer_w_q[i].astype(BF16), peer_k1[i].astype(BF16),
                                     peer_k2[i].astype(BF16))
        h2 = _peer_main(h1, peer_u[i].astype(BF16), jnp.transpose(peer_v[i]).astype(BF16),
                        s2, e2, th, e1, row(ln2_g[i]), row(ln2_b[i]))
        h = _ple(h2, p[i].reshape(t, PLE_DIM), ple_gate_w[i].astype(BF16), row(ple_gate_b[i]),
                 ple_w[i].astype(BF16)).reshape(b, s, d)
    return h
```

```python
import functools
import math

import numpy as np
import jax
import jax.numpy as jnp
from jax import lax
from jax.experimental import pallas as pl
from jax.experimental.pallas import tpu as pltpu

D_MODEL = 1024
DEPTH = 4
CHUNK = 64
PLE_DIM = 256
N_BUCKETS = 32
MAX_DISTANCE = 256
ROPE_THETA = 10000.0
LN_EPS = 1e-5
RMS_EPS = 1e-6
H_A, A_NOPE, A_ROPE, A_V, A_Q_RANK, A_KV_RANK = 8, 64, 32, 64, 512, 256
A_SCALE = (A_NOPE + A_ROPE) ** -0.5
H_B, B_DH = 8, 32
B_SCALE = B_DH ** -0.5
H_C, KV_C, C_DH = 16, 4, 64
C_GROUP = H_C // KV_C
C_SCALE = C_DH ** -0.5
H_IDX, IDX_DH, IDX_ROPE = 16, 64, 32
IDX_SCALE = IDX_DH ** -0.5
TOPK_MAX = 256
P_HEADS, N_KEYS, P_DQ, P_TOPK = 8, 128, 256, 16
P_DHALF = P_DQ // 2
N_EXPERTS = N_KEYS * N_KEYS
DN_ALPHA = (2 * DEPTH) ** 0.25

LANES = 128
ATT_BLK = 256
ATT_ROWS = 256
ATT_AHEAD = 2
DSA_PAIRS = 2
PEER_TE = 8 * N_KEYS
VMEM_LIMIT = 52 * 1024 * 1024

LOG2E = math.log2(math.e)

BF16 = jnp.bfloat16
F32 = jnp.float32
NEG_INF = float("-inf")
M_INIT = -1e30
INT_MIN = np.int32(-2 ** 31)


def _cparams(*sem):
    return pltpu.CompilerParams(dimension_semantics=sem, vmem_limit_bytes=VMEM_LIMIT)


def _full(shape):
    n = len(shape)
    return pl.BlockSpec(shape, lambda *_: (0,) * n)


def _smem():
    return pl.BlockSpec(memory_space=pltpu.SMEM)


def _rms(v, g):
    return v * lax.rsqrt(jnp.mean(v * v, axis=-1, keepdims=True) + RMS_EPS) * g


def _layer_norm(y, g, b):
    mu = jnp.mean(y, axis=-1, keepdims=True)
    yc = y - mu
    var = jnp.mean(yc * yc, axis=-1, keepdims=True)
    return yc * lax.rsqrt(var + LN_EPS) * g + b


def _dot(a, b):
    return jnp.dot(a, b, preferred_element_type=F32)


def _dot_nt(a, b):
    return lax.dot_general(a, b, (((1,), (1,)), ((), ())), preferred_element_type=F32)


def _even_proj_body(h_ref, cos_ref, sin_ref, w_in_ref, w_uq_ref, w_ukv_ref, qg_ref, kvg_ref,
                    qn_ref, qr_ref, kn_ref, kr_ref, va_ref, qd_ref, kd_ref, vd_ref):
    x = h_ref[0].astype(BF16)
    acc = _dot(x, w_in_ref[...])
    cos = cos_ref[0]
    sin = sin_ref[0]
    cq = _rms(acc[:, 0:512], qg_ref[...]).astype(BF16)
    qa = _dot(cq, w_uq_ref[...])
    qn_ref[0] = (qa[:, 0:512] * (A_SCALE * LOG2E)).astype(BF16)
    cos2 = jnp.concatenate([cos, cos], axis=1)
    sin2 = jnp.concatenate([sin, sin], axis=1)
    qr_ref[0] = ((qa[:, 512:768] * cos2 + qa[:, 768:1024] * sin2) * (A_SCALE * LOG2E)).astype(BF16)
    ckv = _rms(acc[:, 512:768], kvg_ref[...]).astype(BF16)
    kv = _dot(ckv, w_ukv_ref[...])
    ones_hi = (lax.broadcasted_iota(jnp.int32, (1, 1024), 1) % LANES >= LANES // 2).astype(F32)
    kn_ref[0] = kv[:, 0:512].astype(BF16)
    va_ref[0] = (kv[:, 512:1536] + ones_hi).astype(BF16)
    qd_ref[0] = (acc[:, 768:1280] * (B_SCALE * LOG2E)).astype(BF16)
    kd_ref[0] = acc[:, 1280:1792].astype(BF16)
    vd_ref[0] = (acc[:, 1792:2816] + ones_hi).astype(BF16)
    kr_ref[0] = (acc[:, 2816:2944] * cos + acc[:, 2944:3072] * sin).astype(BF16)


def _even_proj(h, cos, sin, w_in, w_uq, w_ukv, qg, kvg):
    b, s, d = h.shape
    tm = min(512, s)
    tok = lambda w: pl.BlockSpec((1, tm, w), lambda i, j: (i, j, 0))
    widths = (512, 256, 512, 128, 1024, 512, 512, 1024)
    return pl.pallas_call(
        _even_proj_body,
        grid=(b, s // tm),
        in_specs=[tok(d), tok(LANES), tok(LANES), _full(w_in.shape), _full(w_uq.shape),
                  _full(w_ukv.shape), _full(qg.shape), _full(kvg.shape)],
        out_specs=[tok(w) for w in widths],
        out_shape=[jax.ShapeDtypeStruct((b, s, w), BF16) for w in widths],
        compiler_params=_cparams("parallel", "parallel"),
        name="even_proj",
    )(h, cos, sin, w_in, w_uq, w_ukv, qg, kvg)


def _flash_update(m_ref, acc_ref, i, rows, s, v):
    m_prev = m_ref[i, rows, :]
    m_new = jnp.maximum(m_prev, jnp.max(s, axis=1, keepdims=True))
    alpha = jnp.exp2(m_prev - m_new)
    p = jnp.exp2(s - jnp.concatenate([m_new] * (s.shape[1] // LANES), axis=1))
    acc_ref[i, rows, :] = alpha * acc_ref[i, rows, :] + _dot(p.astype(BF16), v)
    m_ref[i, rows, :] = m_new


def _row_tiles(blk):
    return [slice(r, r + ATT_ROWS) for r in range(0, blk, ATT_ROWS)]


def _lookahead(tiles, produce, consume):
    pending = [produce(t) for t in tiles[:ATT_AHEAD]]
    for n, t in enumerate(tiles):
        if n + ATT_AHEAD < len(tiles):
            pending.append(produce(tiles[n + ATT_AHEAD]))
        consume(t, pending.pop(0))


def _normalized(acc, hh):
    r = pltpu.roll(acc, LANES // 2, 1)
    return acc / r if hh == 0 else r / acc


def _chunk_allowed(tq, tk):
    r = lax.broadcasted_iota(jnp.int32, (tq, tk), 0) // CHUNK
    c = lax.broadcasted_iota(jnp.int32, (tq, tk), 1) // CHUNK
    return c <= r


def _even_attn_body(lam_ref, bfar_ref, qn_ref, qr_ref, kn_ref, kr_ref, va_ref, qd_ref, kd_ref, vd_ref,
                    bias_ref, sg_ref, oa_ref, od_ref, m_ref, acc_ref, *, blk, lam_init):
    p = pl.program_id(1)
    qi = pl.program_id(2)
    lane = lax.broadcasted_iota(jnp.int32, (blk, LANES), 1)
    qn = qn_ref[0]
    qr = qr_ref[0]
    qd = qd_ref[0]
    zero = jnp.zeros_like(qn)
    streams = []
    for hh in range(2):
        lo = 64 * hh
        r0 = 32 * (2 * (p % 2) + hh)
        qn_m = jnp.where((lane >= lo) & (lane < lo + 64), qn, zero)
        qr_m = jnp.where((lane >= r0) & (lane < r0 + 32), qr, zero)
        streams.append((jnp.concatenate([qn_m, qr_m], axis=1), "a", hh))
        streams.append((jnp.where((lane >= lo) & (lane < lo + 32), qd, zero), "d", hh))
        streams.append((jnp.where((lane >= lo + 32) & (lane < lo + 64), qd, zero), "d", hh))
    allowed = _chunk_allowed(blk, blk)

    m_ref[...] = jnp.full(m_ref.shape, M_INIT, F32)
    acc_ref[...] = jnp.zeros(acc_ref.shape, F32)

    def block(j, near):
        ks = pl.multiple_of(j * blk, blk)
        ka = jnp.concatenate([kn_ref[0, pl.ds(ks, blk), :], kr_ref[0, pl.ds(ks, blk), :]], axis=1)
        kd = kd_ref[0, pl.ds(ks, blk), :]
        def scores(tile):
            i, rows = tile
            q, kind, hh = streams[i]
            if kind == "a":
                s = _dot_nt(q[rows], ka)
            else:
                s = _dot_nt(q[rows], kd)
                s = s + (bfar_ref[2 * p + hh] if near is None else bias_ref[0, hh, near, rows, :])
            if near == 0:
                s = jnp.where(allowed[rows], s, NEG_INF)
            return s

        def update(tile, s):
            i, rows = tile
            _, kind, hh = streams[i]
            v_ref = va_ref if kind == "a" else vd_ref
            _flash_update(m_ref, acc_ref, i, rows, s, v_ref[0, pl.ds(ks, blk), hh * LANES:(hh + 1) * LANES])

        _lookahead([(i, rows) for i in range(len(streams)) for rows in _row_tiles(blk)], scores, update)

    def far(j, carry):
        block(j, None)
        return carry

    lax.fori_loop(0, jnp.maximum(qi - 1, 0), far, 0)

    @pl.when(qi >= 1)
    def _previous():
        block(qi - 1, 1)

    block(qi, 0)

    lam = lam_ref[0]
    first = lane < 64
    o = [_normalized(acc_ref[i], hh) for i, (_, _, hh) in enumerate(streams)]
    oa_ref[0] = jnp.where(first, o[0], o[3]).astype(BF16)
    od = jnp.where(first, o[1] - lam * o[2], o[4] - lam * o[5])
    sq = od * od
    ms0 = jnp.sum(jnp.where(first, sq, 0.0), axis=1, keepdims=True)
    ms1 = jnp.sum(jnp.where(first, 0.0, sq), axis=1, keepdims=True)
    ms = jnp.where(first, ms0, ms1) * (1.0 / (2 * B_DH))
    od_ref[0] = (od * lax.rsqrt(ms + RMS_EPS) * sg_ref[...] * (1.0 - lam_init)).astype(BF16)


def _even_attn(lam, bfar, qn, qr, kn, kr, va, qd, kd, vd, bias, sg, lam_init):
    b, s, _ = qn.shape
    blk = ATT_BLK
    qblk = lambda f: pl.BlockSpec((1, blk, LANES), f)
    kv = lambda f: pl.BlockSpec((1, s, LANES), f)
    vv = lambda f: pl.BlockSpec((1, s, 2 * LANES), f)
    body = functools.partial(_even_attn_body, blk=blk, lam_init=lam_init)
    return pl.pallas_call(
        body,
        grid=(b, H_A // 2, s // blk),
        in_specs=[_smem(), _smem(),
                  qblk(lambda i, p, q: (i, q, p)), qblk(lambda i, p, q: (i, q, p // 2)),
                  kv(lambda i, p, q: (i, 0, p)), kv(lambda i, p, q: (i, 0, 0)), vv(lambda i, p, q: (i, 0, p)),
                  qblk(lambda i, p, q: (i, q, p)), kv(lambda i, p, q: (i, 0, p)), vv(lambda i, p, q: (i, 0, p)),
                  pl.BlockSpec((1, 2, 2, blk, blk), lambda i, p, q: (p, 0, 0, 0, 0)),
                  _full(sg.shape)],
        out_specs=[qblk(lambda i, p, q: (i, q, p)), qblk(lambda i, p, q: (i, q, p))],
        out_shape=[jax.ShapeDtypeStruct((b, s, 512), BF16)] * 2,
        scratch_shapes=[pltpu.VMEM((6, blk, LANES), F32), pltpu.VMEM((6, blk, LANES), F32)],
        compiler_params=_cparams("parallel", "parallel", "arbitrary"),
        name="even_attn",
    )(lam, bfar, qn, qr, kn, kr, va, qd, kd, vd, bias, sg)


def _oproj_ln_body(*refs, n_in):
    o_refs = refs[:n_in]
    w_refs = refs[n_in:2 * n_in]
    h_ref, g_ref, b_ref, out_ref = refs[2 * n_in:]
    mix = _dot(o_refs[0][...], w_refs[0][...])
    for o_ref, w_ref in zip(o_refs[1:], w_refs[1:]):
        mix = mix + _dot(o_ref[...], w_ref[...])
    out_ref[...] = _layer_norm(DN_ALPHA * h_ref[...] + mix, g_ref[...], b_ref[...])


def _oproj_ln(os_, ws, h, g, b):
    t, d = h.shape
    tm = min(512, t)
    tok = lambda w: pl.BlockSpec((tm, w), lambda i: (i, 0))
    return pl.pallas_call(
        functools.partial(_oproj_ln_body, n_in=len(os_)),
        grid=(t // tm,),
        in_specs=[tok(o.shape[1]) for o in os_] + [_full(w.shape) for w in ws]
                 + [tok(d), _full(g.shape), _full(b.shape)],
        out_specs=tok(d),
        out_shape=jax.ShapeDtypeStruct((t, d), F32),
        compiler_params=_cparams("parallel"),
        name="oproj_ln",
    )(*os_, *ws, h, g, b)


def _odd_proj_body(h_ref, cos_ref, sin_ref, w_ref, q_ref, k_ref, v_ref, qi_ref, ki_ref, wi_ref):
    x = h_ref[0].astype(BF16)
    acc = _dot(x, w_ref[...])
    cos = cos_ref[0]
    sin = sin_ref[0]
    q_ref[0] = (acc[:, 0:1024] * (C_SCALE * LOG2E)).astype(BF16)
    k_ref[0] = acc[:, 1024:1280].astype(BF16)
    ones_hi = (lax.broadcasted_iota(jnp.int32, (1, 512), 1) % LANES >= LANES // 2).astype(F32)
    v_ref[0] = (acc[:, 1280:1792] + ones_hi).astype(BF16)
    cos8 = jnp.concatenate([cos] * 8, axis=1)
    sin8 = jnp.concatenate([sin] * 8, axis=1)
    qi_ref[0] = ((acc[:, 1792:2816] * cos8 + acc[:, 2816:3840] * sin8) * IDX_SCALE).astype(BF16)
    ki_ref[0] = (acc[:, 3840:3968] * cos + acc[:, 3968:4096] * sin).astype(BF16)
    wi_ref[0] = acc[:, 4096:4224] * (H_IDX ** -0.5)


def _odd_proj(h, cos, sin, w):
    b, s, d = h.shape
    tm = min(256, s)
    tok = lambda wd: pl.BlockSpec((1, tm, wd), lambda i, j: (i, j, 0))
    widths = (1024, 256, 512, 1024, 128, 128)
    dts = (BF16, BF16, BF16, BF16, BF16, F32)
    return pl.pallas_call(
        _odd_proj_body,
        grid=(b, s // tm),
        in_specs=[tok(d), tok(LANES), tok(LANES), _full(w.shape)],
        out_specs=[tok(wd) for wd in widths],
        out_shape=[jax.ShapeDtypeStruct((b, s, wd), dt) for wd, dt in zip(widths, dts)],
        compiler_params=_cparams("parallel", "parallel"),
        name="odd_proj",
    )(h, cos, sin, w)


def _dsa_body(bfar_ref, qi_ref, ki_ref, wi_ref, q_ref, k_ref, v0_ref, v1_ref, bias_ref, o_ref,
              key_ref, madd_ref, m_ref, acc_ref, *, blk, n_sel):
    qb = pl.program_id(1)
    hb = pl.program_id(2)
    lane = lax.broadcasted_iota(jnp.int32, (blk, LANES), 1)
    first = lane < 64

    @pl.when(hb == 0)
    def _select():
        wt = jnp.transpose(wi_ref[0])
        kc = lax.broadcasted_iota(jnp.int32, (blk, blk), 0) // CHUNK
        qc = lax.broadcasted_iota(jnp.int32, (blk, blk), 1) // CHUNK
        chunk_mask = jnp.where(kc <= qc, 0.0, NEG_INF)
        qis = []
        for hp in range(H_IDX // 2):
            qp = qi_ref[0, :, hp * LANES:(hp + 1) * LANES]
            qis.append(jnp.where(first, qp, jnp.zeros_like(qp)))
            qis.append(jnp.where(first, jnp.zeros_like(qp), qp))

        def score(j, _):
            ks = pl.multiple_of(j * blk, blk)
            for rows in (slice(0, blk // 2), slice(blk // 2, blk)):
                kib = ki_ref[0, pl.ds(ks + rows.start, blk // 2), :]
                sc = jnp.zeros((blk // 2, blk), F32)
                for h in range(H_IDX):
                    sc = sc + jnp.maximum(_dot_nt(kib, qis[h]), 0.0) * wt[h:h + 1, :]
                sc = sc + jnp.where(j < qb, 0.0, chunk_mask[rows])
                bits = pltpu.bitcast(sc, jnp.int32)
                key_ref[j, rows, :] = jnp.where(bits < 0, bits ^ np.int32(0x7FFFFFFF), bits)
            return 0

        lax.fori_loop(0, qb + 1, score, 0)

        def count_ge(cand):
            def body(j, part):
                sel = jnp.where(key_ref[j] >= cand, 1.0, 0.0)
                return part + jnp.sum(sel.reshape(blk // 8, 8, blk), axis=0)
            part = lax.fori_loop(0, qb + 1, body, jnp.zeros((8, blk), F32))
            return jnp.sum(part, axis=0, keepdims=True)

        n = float(n_sel)
        zero_i = jnp.zeros((1, blk), jnp.int32)
        ans = jnp.where(count_ge(zero_i) >= n, zero_i, zero_i + INT_MIN)

        def bit_step(t, ans):
            cand = ans | lax.shift_left(jnp.int32(1), 30 - t)
            return jnp.where(count_ge(cand) >= n, cand, ans)

        thr = lax.fori_loop(0, 31, bit_step, ans)

        def make_mask(j, _):
            sel = jnp.where(key_ref[j] >= thr, 0.0, NEG_INF) + jnp.where(j < qb, 0.0, chunk_mask)
            madd_ref[j] = jnp.transpose(sel)
            return 0

        lax.fori_loop(0, qb + 1, make_mask, 0)

    heads = []
    for g in range(DSA_PAIRS):
        q = q_ref[0, :, g * LANES:(g + 1) * LANES]
        zq = jnp.zeros_like(q)
        heads.append((jnp.where(first, q, zq), g, 0))
        heads.append((jnp.where(first, zq, q), g, 1))
    v_refs = (v0_ref, v1_ref)
    m_ref[...] = jnp.full(m_ref.shape, M_INIT, F32)
    acc_ref[...] = jnp.zeros(acc_ref.shape, F32)

    def block(j, near):
        ks = pl.multiple_of(j * blk, blk)
        kb = k_ref[0, pl.ds(ks, blk), :]

        def scores(tile):
            i, rows = tile
            q, g, hh = heads[i]
            s = _dot_nt(q[rows], kb) + madd_ref[j, rows, :]
            if near is None:
                return s + bfar_ref[2 * DSA_PAIRS * hb + 2 * g + hh]
            return s + bias_ref[0, g, hh, near, rows, :]

        def update(tile, s):
            i, rows = tile
            _flash_update(m_ref, acc_ref, i, rows, s, v_refs[heads[i][2]][0, pl.ds(ks, blk), :])

        _lookahead([(i, rows) for i in range(len(heads)) for rows in _row_tiles(blk)], scores, update)

    def far(j, carry):
        block(j, None)
        return carry

    lax.fori_loop(0, jnp.maximum(qb - 1, 0), far, 0)

    @pl.when(qb >= 1)
    def _previous():
        block(qb - 1, 1)

    block(qb, 0)
    for g in range(DSA_PAIRS):
        o = jnp.where(first, _normalized(acc_ref[2 * g], 0), _normalized(acc_ref[2 * g + 1], 1))
        o_ref[0, :, g * LANES:(g + 1) * LANES] = o.astype(BF16)


def _dsa_attn(bfar, qi, ki, wi, q, k, v, bias, n_sel):
    b, s, _ = q.shape
    blk = ATT_BLK
    nk = s // blk
    nhb = H_C // (2 * DSA_PAIRS)
    width = DSA_PAIRS * LANES
    kvp = lambda hb: hb * DSA_PAIRS // C_GROUP
    bias = bias.reshape(nhb, DSA_PAIRS, 2, 2, blk, blk)
    body = functools.partial(_dsa_body, blk=blk, n_sel=n_sel)
    return pl.pallas_call(
        body,
        grid=(b, nk, nhb),
        in_specs=[_smem(),
                  pl.BlockSpec((1, blk, 1024), lambda i, qb, hb: (i, qb, 0)),
                  pl.BlockSpec((1, s, LANES), lambda i, qb, hb: (i, 0, 0)),
                  pl.BlockSpec((1, blk, LANES), lambda i, qb, hb: (i, qb, 0)),
                  pl.BlockSpec((1, blk, width), lambda i, qb, hb: (i, qb, hb)),
                  pl.BlockSpec((1, s, LANES), lambda i, qb, hb: (i, 0, kvp(hb))),
                  pl.BlockSpec((1, s, LANES), lambda i, qb, hb: (i, 0, 2 * kvp(hb))),
                  pl.BlockSpec((1, s, LANES), lambda i, qb, hb: (i, 0, 2 * kvp(hb) + 1)),
                  pl.BlockSpec((1, DSA_PAIRS, 2, 2, blk, blk), lambda i, qb, hb: (hb, 0, 0, 0, 0, 0))],
        out_specs=pl.BlockSpec((1, blk, width), lambda i, qb, hb: (i, qb, hb)),
        out_shape=jax.ShapeDtypeStruct((b, s, H_C * C_DH), BF16),
        scratch_shapes=[pltpu.VMEM((nk, blk, blk), jnp.int32),
                        pltpu.VMEM((nk, blk, blk), F32),
                        pltpu.VMEM((2 * DSA_PAIRS, blk, LANES), F32),
                        pltpu.VMEM((2 * DSA_PAIRS, blk, LANES), F32)],
        compiler_params=_cparams("parallel", "arbitrary", "arbitrary"),
        name="dsa_attn",
    )(bfar, qi, ki, wi, q, k, v, v, bias)


def _top_rows(cur, n):
    vals = []
    for _ in range(n):
        m = jnp.max(cur, axis=0, keepdims=True)
        vals.append(m)
        cur = jnp.where(cur == m, NEG_INF, cur)
    return vals


def _peer_route_body(h_ref, wq_ref, k1_ref, k2_ref, s2_ref, e2_ref, th_ref, e1_ref, q_scr):
    tm = h_ref.shape[0]
    q = _dot(h_ref[...].astype(BF16), wq_ref[...]).astype(BF16)
    for h in range(P_HEADS):
        q_scr[h] = q[:, h * P_DQ:(h + 1) * P_DQ]
    row16 = lax.broadcasted_iota(jnp.int32, (P_TOPK, tm), 0)

    def head(h):
        qh = q_scr[h]
        s1 = _dot_nt(k1_ref[...], qh[:, 0:P_DHALF])
        s2 = _dot_nt(k2_ref[...], qh[:, P_DHALF:P_DQ])
        t1 = _top_rows(s1, P_TOPK + 1)
        t2 = _top_rows(s2, P_TOPK + 1)
        t1m = jnp.zeros((P_TOPK, tm), F32)
        t2m = jnp.zeros((P_TOPK, tm), F32)
        for r in range(P_TOPK):
            t1m = jnp.where(row16 == r, t1[r], t1m)
            t2m = jnp.where(row16 == r, t2[r], t2m)
        cand = jnp.concatenate([t1[r] + t2m for r in range(4)]
                               + [t1[r] + t2m[0:8] for r in range(4, 8)]
                               + [t1m[8:16] + t2[0]], axis=0)
        c = _top_rows(cand, P_TOPK + 1)
        c17 = jnp.maximum(c[P_TOPK], jnp.maximum(t1[P_TOPK] + t2[0], t1[0] + t2[P_TOPK]))
        tau = 0.5 * (c[P_TOPK - 1] + c17)
        z = jnp.sum(jnp.where(cand >= tau, jnp.exp(cand - (t1[0] + t2[0])), 0.0), axis=0, keepdims=True)
        outs = ((s2_ref, s2), (e2_ref, jnp.exp(s2 - t2[0])), (th_ref, tau - s1),
                (e1_ref, jnp.exp(s1 - t1[0]) / z))
        for ref, val in outs:
            for c in range(tm // LANES):
                ref[c, h] = val[:, c * LANES:(c + 1) * LANES]

    def head_pair(hp, carry):
        head(2 * hp)
        head(2 * hp + 1)
        return carry

    lax.fori_loop(0, P_HEADS // 2, head_pair, 0)


def _peer_route(h1, wq, k1, k2):
    t, d = h1.shape
    tm = min(256, t)
    out = pl.BlockSpec((tm // LANES, P_HEADS, N_KEYS, LANES), lambda i: (i, 0, 0, 0))
    return pl.pallas_call(
        _peer_route_body,
        grid=(t // tm,),
        in_specs=[pl.BlockSpec((tm, d), lambda i: (i, 0)), _full(wq.shape), _full(k1.shape), _full(k2.shape)],
        out_specs=[out] * 4,
        out_shape=[jax.ShapeDtypeStruct((t // LANES, P_HEADS, N_KEYS, LANES), F32)] * 4,
        scratch_shapes=[pltpu.VMEM((P_HEADS, tm, P_DQ), BF16)],
        compiler_params=_cparams("parallel"),
        name="peer_route",
    )(h1, wq, k1, k2)


def _peer_main_body(h1_ref, u_ref, vt_ref, s2_ref, e2_ref, th_ref, e1_ref, g_ref, b_ref, out_ref,
                    xt_ref, acc_ref, at_ref, gw_ref):
    e = pl.program_id(1)
    tm = h1_ref.shape[0]
    nsl = PEER_TE // N_KEYS

    @pl.when(e == 0)
    def _init():
        xt_ref[...] = jnp.transpose(h1_ref[...]).astype(BF16)
        acc_ref[...] = jnp.zeros_like(acc_ref)

    i0 = pl.multiple_of(e * nsl, nsl)
    quarter = 2 * N_KEYS

    def up_proj(q):
        rows = slice(q * quarter, (q + 1) * quarter)
        at_ref[rows, :] = _dot(u_ref[rows, :], xt_ref[...])

    def gate(q):
        slabs = (2 * q, 2 * q + 1)
        for c in range(tm // LANES):
            ls = slice(c * LANES, (c + 1) * LANES)
            ws = [jnp.zeros((N_KEYS, LANES), F32) for _ in slabs]
            for h in range(P_HEADS):
                s2t = s2_ref[c, h]
                e2t = e2_ref[c, h]
                for k, ii in enumerate(slabs):
                    th = th_ref[c, h, pl.ds(i0 + ii, 1), :]
                    e1 = e1_ref[c, h, pl.ds(i0 + ii, 1), :]
                    ws[k] = ws[k] + jnp.where(s2t >= th, e2t, 0.0) * e1
            for k, ii in enumerate(slabs):
                rows = slice(ii * N_KEYS, (ii + 1) * N_KEYS)
                gw_ref[rows, ls] = (jax.nn.gelu(at_ref[rows, ls]) * ws[k]).astype(BF16)

    def down_proj(q):
        rows = slice(q * quarter, (q + 1) * quarter)
        acc_ref[...] += _dot(vt_ref[:, rows], gw_ref[rows, :])

    up_proj(0)
    up_proj(1)
    for q in range(4):
        gate(q)
        down_proj(q)
        if q + 2 < 4:
            up_proj(q + 2)

    @pl.when(e == pl.num_programs(1) - 1)
    def _finish():
        y = DN_ALPHA * h1_ref[...] + jnp.transpose(acc_ref[...])
        out_ref[...] = _layer_norm(y, g_ref[...], b_ref[...])


def _peer_main(h1, u, vt, s2, e2, th, e1, g, b):
    t, d = h1.shape
    tm = min(512, t)
    route = pl.BlockSpec((tm // LANES, P_HEADS, N_KEYS, LANES), lambda i, e: (i, 0, 0, 0))
    return pl.pallas_call(
        _peer_main_body,
        grid=(t // tm, N_EXPERTS // PEER_TE),
        in_specs=[pl.BlockSpec((tm, d), lambda i, e: (i, 0)),
                  pl.BlockSpec((PEER_TE, d), lambda i, e: (e, 0)),
                  pl.BlockSpec((d, PEER_TE), lambda i, e: (0, e)),
                  route, route, route, route, _full(g.shape), _full(b.shape)],
        out_specs=pl.BlockSpec((tm, d), lambda i, e: (i, 0)),
        out_shape=jax.ShapeDtypeStruct((t, d), F32),
        scratch_shapes=[pltpu.VMEM((d, tm), BF16), pltpu.VMEM((d, tm), F32),
                        pltpu.VMEM((PEER_TE, tm), F32), pltpu.VMEM((PEER_TE, tm), BF16)],
        compiler_params=_cparams("parallel", "arbitrary"),
        name="peer_main",
    )(h1, u, vt, s2, e2, th, e1, g, b)


def _ple_body(h_ref, p_ref, gw_ref, gb_ref, pw_ref, out_ref):
    h = h_ref[...]
    gate = jax.nn.sigmoid(_dot(h.astype(BF16), gw_ref[...]) + gb_ref[...])
    out_ref[...] = h + gate * _dot(p_ref[...].astype(BF16), pw_ref[...])


def _ple(h, p, gw, gb, pw):
    t, d = h.shape
    tm = min(512, t)
    return pl.pallas_call(
        _ple_body,
        grid=(t // tm,),
        in_specs=[pl.BlockSpec((tm, d), lambda i: (i, 0)), pl.BlockSpec((tm, PLE_DIM), lambda i: (i, 0)),
                  _full(gw.shape), _full(gb.shape), _full(pw.shape)],
        out_specs=pl.BlockSpec((tm, d), lambda i: (i, 0)),
        out_shape=jax.ShapeDtypeStruct((t, d), F32),
        compiler_params=_cparams("parallel"),
        name="ple",
    )(h, p, gw, gb, pw)


def _t5_bucket(rel):
    nb = N_BUCKETS // 2
    max_exact = nb // 2
    n = jnp.abs(rel)
    large = max_exact + (jnp.log(jnp.maximum(n, 1).astype(F32) / max_exact)
                         / math.log(MAX_DISTANCE / max_exact) * (nb - max_exact)).astype(jnp.int32)
    large = jnp.minimum(large, nb - 1)
    return jnp.where(rel > 0, nb, 0) + jnp.where(n < max_exact, n, large)


def _bias_tables(tab, heads, blk):
    r = jnp.arange(blk, dtype=jnp.int32)[:, None]
    c = jnp.arange(blk, dtype=jnp.int32)[None, :]
    rel = jnp.stack([c - r, c - r - blk])
    onehot = (_t5_bucket(rel).reshape(1, -1) == jnp.arange(N_BUCKETS, dtype=jnp.int32)[:, None]).astype(F32)
    tiles = jnp.dot(tab[:, jnp.asarray(heads)].T.astype(F32), onehot, precision=lax.Precision.HIGHEST)
    tiles = tiles.reshape(len(heads) // 2, 2, 2, blk, blk)
    far = tab[_t5_bucket(jnp.int32(-(blk + 1)))][jnp.asarray(heads)]
    return tiles.astype(F32) * LOG2E, far.astype(F32) * LOG2E


def _pad_heads(w, n_heads):
    rows = w.shape[0]
    w = w.reshape(rows, n_heads, LANES // 2)
    return jnp.concatenate([w, jnp.zeros_like(w)], axis=-1).reshape(rows, n_heads * LANES)


def _swap_half(w):
    half = w.shape[-1] // 2
    return jnp.concatenate([-w[..., half:], w[..., :half]], axis=-1)


def _even_weights(w_in, w_uq, w_ukv):
    c_q, c_kv, k_r = w_in[:, 0:512], w_in[:, 512:768], w_in[:, 768:800]
    qk_d, v_d = w_in[:, 800:1824], w_in[:, 1824:2336]
    w_in2 = jnp.concatenate([c_q, c_kv, qk_d, _pad_heads(v_d, H_B), jnp.tile(k_r, (1, 4)),
                             jnp.tile(_swap_half(k_r), (1, 4))], axis=1)
    uq = w_uq.reshape(A_Q_RANK, H_A, A_NOPE + A_ROPE)
    rope = uq[:, :, A_NOPE:]
    w_uq2 = jnp.concatenate([uq[:, :, :A_NOPE].reshape(A_Q_RANK, -1), rope.reshape(A_Q_RANK, -1),
                             _swap_half(rope).reshape(A_Q_RANK, -1)], axis=1)
    ukv = w_ukv.reshape(A_KV_RANK, H_A, A_NOPE + A_V)
    w_ukv2 = jnp.concatenate([ukv[:, :, :A_NOPE].reshape(A_KV_RANK, -1),
                              _pad_heads(ukv[:, :, A_NOPE:].reshape(A_KV_RANK, -1), H_A)], axis=1)
    return w_in2.astype(BF16), w_uq2.astype(BF16), w_ukv2.astype(BF16)


_ODD_HEADS = [(2 * pp + a) * C_GROUP + g for pp in range(KV_C // 2) for g in range(C_GROUP) for a in range(2)]


def _odd_weights(w_in, w_o):
    d = w_in.shape[0]
    heads = jnp.asarray(_ODD_HEADS)
    wq = w_in[:, 0:1024].reshape(d, H_C, C_DH)[:, heads].reshape(d, -1)
    wk, wv = w_in[:, 1024:1280], w_in[:, 1280:1536]
    wqi = w_in[:, 1536:2560].reshape(d, H_IDX, IDX_DH)
    wqi_s = jnp.concatenate([_swap_half(wqi[:, :, :IDX_ROPE]), jnp.zeros_like(wqi[:, :, IDX_ROPE:])], axis=-1)
    wki = w_in[:, 2560:2624]
    wki_s = jnp.concatenate([_swap_half(wki[:, :IDX_ROPE]), jnp.zeros_like(wki[:, IDX_ROPE:])], axis=-1)
    wwi = jnp.pad(w_in[:, 2624:2640], ((0, 0), (0, LANES - H_IDX)))
    w2 = jnp.concatenate([wq, wk, _pad_heads(wv, KV_C), wqi.reshape(d, -1), wqi_s.reshape(d, -1),
                          jnp.tile(wki, (1, 2)), jnp.tile(wki_s, (1, 2)), wwi], axis=1)
    w_o2 = w_o.reshape(H_C, C_DH, -1)[heads].reshape(H_C * C_DH, -1)
    return w2.astype(BF16), w_o2.astype(BF16)


def _rope_tables(positions):
    half = A_ROPE // 2
    freqs = ROPE_THETA ** (-jnp.arange(half, dtype=F32) / half)
    ang = positions.astype(F32)[..., None] * freqs
    cos, sin = jnp.cos(ang), jnp.sin(ang)
    cos32 = jnp.concatenate([cos, cos], axis=-1)
    sin32 = jnp.concatenate([sin, sin], axis=-1)
    ones, zeros = jnp.ones_like(cos32), jnp.zeros_like(sin32)
    return (jnp.tile(cos32, (1, 1, 4)), jnp.tile(sin32, (1, 1, 4)),
            jnp.tile(jnp.concatenate([cos32, ones], axis=-1), (1, 1, 2)),
            jnp.tile(jnp.concatenate([sin32, zeros], axis=-1), (1, 1, 2)))


def kernel(x, p, positions, rel_bias, ev_w_in, ev_w_uq, ev_w_ukv, ev_q_norm, ev_kv_norm, ev_lam_q1, ev_lam_k1, ev_lam_q2, ev_lam_k2, ev_subln, ev_w_o, od_w_in, od_w_o, ln1_g, ln1_b, ln2_g, ln2_b, peer_w_q, peer_k1, peer_k2, peer_u, peer_v, ple_w, ple_gate_w, ple_gate_b):
    b, s, d = x.shape
    t = b * s
    assert s % ATT_BLK == 0 and d == D_MODEL
    n_sel = min(TOPK_MAX, s // 4)
    cos_f, sin_f, cos_p, sin_p = _rope_tables(positions)
    bias_b, bfar_b = _bias_tables(rel_bias[:, :H_B], list(range(H_B)), ATT_BLK)
    bias_c, bfar_c = _bias_tables(rel_bias[:, H_B:], _ODD_HEADS, ATT_BLK)
    row = lambda a: a.reshape(1, -1).astype(F32)

    h = x
    for i in range(DEPTH):
        j = i // 2
        if i % 2 == 0:
            lam_init = 0.8 - 0.6 * math.exp(-0.3 * i)
            lam = (jnp.exp(jnp.sum(ev_lam_q1[j] * ev_lam_k1[j], dtype=F32))
                   - jnp.exp(jnp.sum(ev_lam_q2[j] * ev_lam_k2[j], dtype=F32)) + lam_init).reshape(1)
            w_in2, w_uq2, w_ukv2 = _even_weights(ev_w_in[j], ev_w_uq[j], ev_w_ukv[j])
            qn, qr, kn, kr, va, qd, kd, vd = _even_proj(h, cos_f, sin_f, w_in2, w_uq2, w_ukv2,
                                                        row(ev_q_norm[j]), row(ev_kv_norm[j]))
            sg = row(jnp.tile(ev_subln[j], 2))
            oa, od = _even_attn(lam, bfar_b, qn, qr, kn, kr, va, qd, kd, vd, bias_b, sg, lam_init)
            w_o = ev_w_o[j].astype(BF16)
            os_ = [oa.reshape(t, -1), od.reshape(t, -1)]
            ws = [w_o[:H_A * A_V], w_o[H_A * A_V:]]
        else:
            w_in2, w_o2 = _odd_weights(od_w_in[j], od_w_o[j])
            q, k, v, qi, ki, wi = _odd_proj(h, cos_p, sin_p, w_in2)
            o = _dsa_attn(bfar_c, qi, ki, wi, q, k, v, bias_c, n_sel)
            os_, ws = [o.reshape(t, -1)], [w_o2]
        h1 = _oproj_ln(os_, ws, h.reshape(t, d), row(ln1_g[i]), row(ln1_b[i]))
        s2, e2, th, e1 = _peer_route(h1, peer_w_q[i].astype(BF16), peer_k1[i].astype(BF16),
                                     peer_k2[i].astype(BF16))
        h2 = _peer_main(h1, peer_u[i].astype(BF16), jnp.transpose(peer_v[i]).astype(BF16),
                        s2, e2, th, e1, row(ln2_g[i]), row(ln2_b[i]))
        h = _ple(h2, p[i].reshape(t, PLE_DIM), ple_gate_w[i].astype(BF16), row(ple_gate_b[i]),
                 ple_w[i].astype(BF16)).reshape(b, s, d)
    return h
```

```python
import functools
import math

import numpy as np
import jax
import jax.numpy as jnp
from jax import lax
from jax.experimental import pallas as pl
from jax.experimental.pallas import tpu as pltpu

D_MODEL = 1024
DEPTH = 4
CHUNK = 64
PLE_DIM = 256
N_BUCKETS = 32
MAX_DISTANCE = 256
ROPE_THETA = 10000.0
LN_EPS = 1e-5
RMS_EPS = 1e-6
H_A, A_NOPE, A_ROPE, A_V, A_Q_RANK, A_KV_RANK = 8, 64, 32, 64, 512, 256
A_SCALE = (A_NOPE + A_ROPE) ** -0.5
H_B, B_DH = 8, 32
B_SCALE = B_DH ** -0.5
H_C, KV_C, C_DH = 16, 4, 64
C_GROUP = H_C // KV_C
C_SCALE = C_DH ** -0.5
H_IDX, IDX_DH, IDX_ROPE = 16, 64, 32
IDX_SCALE = IDX_DH ** -0.5
TOPK_MAX = 256
P_HEADS, N_KEYS, P_DQ, P_TOPK = 8, 128, 256, 16
P_DHALF = P_DQ // 2
N_EXPERTS = N_KEYS * N_KEYS
DN_ALPHA = (2 * DEPTH) ** 0.25

LANES = 128
ATT_BLK = 256
ATT_ROWS = 256
ATT_AHEAD = 2
DSA_PAIRS = 4
PEER_TE = 8 * N_KEYS
VMEM_LIMIT = 52 * 1024 * 1024

LOG2E = math.log2(math.e)

BF16 = jnp.bfloat16
F32 = jnp.float32
NEG_INF = float("-inf")
M_INIT = -1e30
INT_MIN = np.int32(-2 ** 31)


def _cparams(*sem):
    return pltpu.CompilerParams(dimension_semantics=sem, vmem_limit_bytes=VMEM_LIMIT)


def _full(shape):
    n = len(shape)
    return pl.BlockSpec(shape, lambda *_: (0,) * n)


def _smem():
    return pl.BlockSpec(memory_space=pltpu.SMEM)


def _rms(v, g):
    return v * lax.rsqrt(jnp.mean(v * v, axis=-1, keepdims=True) + RMS_EPS) * g


def _layer_norm(y, g, b):
    mu = jnp.mean(y, axis=-1, keepdims=True)
    yc = y - mu
    var = jnp.mean(yc * yc, axis=-1, keepdims=True)
    return yc * lax.rsqrt(var + LN_EPS) * g + b


def _dot(a, b):
    return jnp.dot(a, b, preferred_element_type=F32)


def _dot_nt(a, b):
    return lax.dot_general(a, b, (((1,), (1,)), ((), ())), preferred_element_type=F32)


def _even_proj_body(h_ref, cos_ref, sin_ref, w_in_ref, w_uq_ref, w_ukv_ref, qg_ref, kvg_ref,
                    qn_ref, qr_ref, kn_ref, kr_ref, va_ref, qd_ref, kd_ref, vd_ref):
    x = h_ref[0].astype(BF16)
    acc = _dot(x, w_in_ref[...])
    cos = cos_ref[0]
    sin = sin_ref[0]
    cq = _rms(acc[:, 0:512], qg_ref[...]).astype(BF16)
    qa = _dot(cq, w_uq_ref[...])
    qn_ref[0] = (qa[:, 0:512] * (A_SCALE * LOG2E)).astype(BF16)
    cos2 = jnp.concatenate([cos, cos], axis=1)
    sin2 = jnp.concatenate([sin, sin], axis=1)
    qr_ref[0] = ((qa[:, 512:768] * cos2 + qa[:, 768:1024] * sin2) * (A_SCALE * LOG2E)).astype(BF16)
    ckv = _rms(acc[:, 512:768], kvg_ref[...]).astype(BF16)
    kv = _dot(ckv, w_ukv_ref[...])
    ones_hi = (lax.broadcasted_iota(jnp.int32, (1, 1024), 1) % LANES >= LANES // 2).astype(F32)
    kn_ref[0] = kv[:, 0:512].astype(BF16)
    va_ref[0] = (kv[:, 512:1536] + ones_hi).astype(BF16)
    qd_ref[0] = (acc[:, 768:1280] * (B_SCALE * LOG2E)).astype(BF16)
    kd_ref[0] = acc[:, 1280:1792].astype(BF16)
    vd_ref[0] = (acc[:, 1792:2816] + ones_hi).astype(BF16)
    kr_ref[0] = (acc[:, 2816:2944] * cos + acc[:, 2944:3072] * sin).astype(BF16)


def _even_proj(h, cos, sin, w_in, w_uq, w_ukv, qg, kvg):
    b, s, d = h.shape
    tm = min(512, s)
    tok = lambda w: pl.BlockSpec((1, tm, w), lambda i, j: (i, j, 0))
    widths = (512, 256, 512, 128, 1024, 512, 512, 1024)
    return pl.pallas_call(
        _even_proj_body,
        grid=(b, s // tm),
        in_specs=[tok(d), tok(LANES), tok(LANES), _full(w_in.shape), _full(w_uq.shape),
                  _full(w_ukv.shape), _full(qg.shape), _full(kvg.shape)],
        out_specs=[tok(w) for w in widths],
        out_shape=[jax.ShapeDtypeStruct((b, s, w), BF16) for w in widths],
        compiler_params=_cparams("parallel", "parallel"),
        name="even_proj",
    )(h, cos, sin, w_in, w_uq, w_ukv, qg, kvg)


def _flash_update(m_ref, acc_ref, i, rows, s, v):
    m_prev = m_ref[i, rows, :]
    m_new = jnp.maximum(m_prev, jnp.max(s, axis=1, keepdims=True))
    alpha = jnp.exp2(m_prev - m_new)
    p = jnp.exp2(s - jnp.concatenate([m_new] * (s.shape[1] // LANES), axis=1))
    acc_ref[i, rows, :] = alpha * acc_ref[i, rows, :] + _dot(p.astype(BF16), v)
    m_ref[i, rows, :] = m_new


def _row_tiles(blk):
    return [slice(r, r + ATT_ROWS) for r in range(0, blk, ATT_ROWS)]


def _lookahead(tiles, produce, consume):
    pending = [produce(t) for t in tiles[:ATT_AHEAD]]
    for n, t in enumerate(tiles):
        if n + ATT_AHEAD < len(tiles):
            pending.append(produce(tiles[n + ATT_AHEAD]))
        consume(t, pending.pop(0))


def _normalized(acc, hh):
    r = pltpu.roll(acc, LANES // 2, 1)
    return acc / r if hh == 0 else r / acc


def _chunk_allowed(tq, tk):
    r = lax.broadcasted_iota(jnp.int32, (tq, tk), 0) // CHUNK
    c = lax.broadcasted_iota(jnp.int32, (tq, tk), 1) // CHUNK
    return c <= r


def _even_attn_body(lam_ref, bfar_ref, qn_ref, qr_ref, kn_ref, kr_ref, va_ref, qd_ref, kd_ref, vd_ref,
                    bias_ref, sg_ref, oa_ref, od_ref, m_ref, acc_ref, *, blk, lam_init):
    p = pl.program_id(1)
    qi = pl.program_id(2)
    lane = lax.broadcasted_iota(jnp.int32, (blk, LANES), 1)
    qn = qn_ref[0]
    qr = qr_ref[0]
    qd = qd_ref[0]
    zero = jnp.zeros_like(qn)
    streams = []
    for hh in range(2):
        lo = 64 * hh
        r0 = 32 * (2 * (p % 2) + hh)
        qn_m = jnp.where((lane >= lo) & (lane < lo + 64), qn, zero)
        qr_m = jnp.where((lane >= r0) & (lane < r0 + 32), qr, zero)
        streams.append((jnp.concatenate([qn_m, qr_m], axis=1), "a", hh))
        streams.append((jnp.where((lane >= lo) & (lane < lo + 32), qd, zero), "d", hh))
        streams.append((jnp.where((lane >= lo + 32) & (lane < lo + 64), qd, zero), "d", hh))
    allowed = _chunk_allowed(blk, blk)

    m_ref[...] = jnp.full(m_ref.shape, M_INIT, F32)
    acc_ref[...] = jnp.zeros(acc_ref.shape, F32)

    def block(j, near):
        ks = pl.multiple_of(j * blk, blk)
        ka = jnp.concatenate([kn_ref[0, pl.ds(ks, blk), :], kr_ref[0, pl.ds(ks, blk), :]], axis=1)
        kd = kd_ref[0, pl.ds(ks, blk), :]
        def scores(tile):
            i, rows = tile
            q, kind, hh = streams[i]
            if kind == "a":
                s = _dot_nt(q[rows], ka)
            else:
                s = _dot_nt(q[rows], kd)
                s = s + (bfar_ref[2 * p + hh] if near is None else bias_ref[0, hh, near, rows, :])
            if near == 0:
                s = jnp.where(allowed[rows], s, NEG_INF)
            return s

        def update(tile, s):
            i, rows = tile
            _, kind, hh = streams[i]
            v_ref = va_ref if kind == "a" else vd_ref
            _flash_update(m_ref, acc_ref, i, rows, s, v_ref[0, pl.ds(ks, blk), hh * LANES:(hh + 1) * LANES])

        _lookahead([(i, rows) for i in range(len(streams)) for rows in _row_tiles(blk)], scores, update)

    def far(j, carry):
        block(j, None)
        return carry

    lax.fori_loop(0, jnp.maximum(qi - 1, 0), far, 0)

    @pl.when(qi >= 1)
    def _previous():
        block(qi - 1, 1)

    block(qi, 0)

    lam = lam_ref[0]
    first = lane < 64
    o = [_normalized(acc_ref[i], hh) for i, (_, _, hh) in enumerate(streams)]
    oa_ref[0] = jnp.where(first, o[0], o[3]).astype(BF16)
    od = jnp.where(first, o[1] - lam * o[2], o[4] - lam * o[5])
    sq = od * od
    ms0 = jnp.sum(jnp.where(first, sq, 0.0), axis=1, keepdims=True)
    ms1 = jnp.sum(jnp.where(first, 0.0, sq), axis=1, keepdims=True)
    ms = jnp.where(first, ms0, ms1) * (1.0 / (2 * B_DH))
    od_ref[0] = (od * lax.rsqrt(ms + RMS_EPS) * sg_ref[...] * (1.0 - lam_init)).astype(BF16)


def _even_attn(lam, bfar, qn, qr, kn, kr, va, qd, kd, vd, bias, sg, lam_init):
    b, s, _ = qn.shape
    blk = ATT_BLK
    qblk = lambda f: pl.BlockSpec((1, blk, LANES), f)
    kv = lambda f: pl.BlockSpec((1, s, LANES), f)
    vv = lambda f: pl.BlockSpec((1, s, 2 * LANES), f)
    body = functools.partial(_even_attn_body, blk=blk, lam_init=lam_init)
    return pl.pallas_call(
        body,
        grid=(b, H_A // 2, s // blk),
        in_specs=[_smem(), _smem(),
                  qblk(lambda i, p, q: (i, q, p)), qblk(lambda i, p, q: (i, q, p // 2)),
                  kv(lambda i, p, q: (i, 0, p)), kv(lambda i, p, q: (i, 0, 0)), vv(lambda i, p, q: (i, 0, p)),
                  qblk(lambda i, p, q: (i, q, p)), kv(lambda i, p, q: (i, 0, p)), vv(lambda i, p, q: (i, 0, p)),
                  pl.BlockSpec((1, 2, 2, blk, blk), lambda i, p, q: (p, 0, 0, 0, 0)),
                  _full(sg.shape)],
        out_specs=[qblk(lambda i, p, q: (i, q, p)), qblk(lambda i, p, q: (i, q, p))],
        out_shape=[jax.ShapeDtypeStruct((b, s, 512), BF16)] * 2,
        scratch_shapes=[pltpu.VMEM((6, blk, LANES), F32), pltpu.VMEM((6, blk, LANES), F32)],
        compiler_params=_cparams("parallel", "parallel", "arbitrary"),
        name="even_attn",
    )(lam, bfar, qn, qr, kn, kr, va, qd, kd, vd, bias, sg)


def _oproj_ln_body(*refs, n_in):
    o_refs = refs[:n_in]
    w_refs = refs[n_in:2 * n_in]
    h_ref, g_ref, b_ref, out_ref = refs[2 * n_in:]
    mix = _dot(o_refs[0][...], w_refs[0][...])
    for o_ref, w_ref in zip(o_refs[1:], w_refs[1:]):
        mix = mix + _dot(o_ref[...], w_ref[...])
    out_ref[...] = _layer_norm(DN_ALPHA * h_ref[...] + mix, g_ref[...], b_ref[...])


def _oproj_ln(os_, ws, h, g, b):
    t, d = h.shape
    tm = min(512, t)
    tok = lambda w: pl.BlockSpec((tm, w), lambda i: (i, 0))
    return pl.pallas_call(
        functools.partial(_oproj_ln_body, n_in=len(os_)),
        grid=(t // tm,),
        in_specs=[tok(o.shape[1]) for o in os_] + [_full(w.shape) for w in ws]
                 + [tok(d), _full(g.shape), _full(b.shape)],
        out_specs=tok(d),
        out_shape=jax.ShapeDtypeStruct((t, d), F32),
        compiler_params=_cparams("parallel"),
        name="oproj_ln",
    )(*os_, *ws, h, g, b)


def _odd_proj_body(h_ref, cos_ref, sin_ref, w_ref, q_ref, k_ref, v_ref, qi_ref, ki_ref, wi_ref):
    x = h_ref[0].astype(BF16)
    acc = _dot(x, w_ref[...])
    cos = cos_ref[0]
    sin = sin_ref[0]
    q_ref[0] = (acc[:, 0:1024] * (C_SCALE * LOG2E)).astype(BF16)
    k_ref[0] = acc[:, 1024:1280].astype(BF16)
    ones_hi = (lax.broadcasted_iota(jnp.int32, (1, 512), 1) % LANES >= LANES // 2).astype(F32)
    v_ref[0] = (acc[:, 1280:1792] + ones_hi).astype(BF16)
    cos8 = jnp.concatenate([cos] * 8, axis=1)
    sin8 = jnp.concatenate([sin] * 8, axis=1)
    qi_ref[0] = ((acc[:, 1792:2816] * cos8 + acc[:, 2816:3840] * sin8) * IDX_SCALE).astype(BF16)
    ki_ref[0] = (acc[:, 3840:3968] * cos + acc[:, 3968:4096] * sin).astype(BF16)
    wi_ref[0] = acc[:, 4096:4224] * (H_IDX ** -0.5)


def _odd_proj(h, cos, sin, w):
    b, s, d = h.shape
    tm = min(256, s)
    tok = lambda wd: pl.BlockSpec((1, tm, wd), lambda i, j: (i, j, 0))
    widths = (1024, 256, 512, 1024, 128, 128)
    dts = (BF16, BF16, BF16, BF16, BF16, F32)
    return pl.pallas_call(
        _odd_proj_body,
        grid=(b, s // tm),
        in_specs=[tok(d), tok(LANES), tok(LANES), _full(w.shape)],
        out_specs=[tok(wd) for wd in widths],
        out_shape=[jax.ShapeDtypeStruct((b, s, wd), dt) for wd, dt in zip(widths, dts)],
        compiler_params=_cparams("parallel", "parallel"),
        name="odd_proj",
    )(h, cos, sin, w)


def _dsa_body(bfar_ref, qi_ref, ki_ref, wi_ref, q_ref, k_ref, v0_ref, v1_ref, bias_ref, o_ref,
              key_ref, madd_ref, m_ref, acc_ref, *, blk, n_sel):
    qb = pl.program_id(1)
    hb = pl.program_id(2)
    lane = lax.broadcasted_iota(jnp.int32, (blk, LANES), 1)
    first = lane < 64

    @pl.when(hb == 0)
    def _select():
        wt = jnp.transpose(wi_ref[0])
        kc = lax.broadcasted_iota(jnp.int32, (blk, blk), 0) // CHUNK
        qc = lax.broadcasted_iota(jnp.int32, (blk, blk), 1) // CHUNK
        chunk_mask = jnp.where(kc <= qc, 0.0, NEG_INF)
        qis = []
        for hp in range(H_IDX // 2):
            qp = qi_ref[0, :, hp * LANES:(hp + 1) * LANES]
            qis.append(jnp.where(first, qp, jnp.zeros_like(qp)))
            qis.append(jnp.where(first, jnp.zeros_like(qp), qp))

        def score(j, _):
            ks = pl.multiple_of(j * blk, blk)
            for rows in (slice(0, blk // 2), slice(blk // 2, blk)):
                kib = ki_ref[0, pl.ds(ks + rows.start, blk // 2), :]
                sc = jnp.zeros((blk // 2, blk), F32)
                for h in range(H_IDX):
                    sc = sc + jnp.maximum(_dot_nt(kib, qis[h]), 0.0) * wt[h:h + 1, :]
                sc = sc + jnp.where(j < qb, 0.0, chunk_mask[rows])
                bits = pltpu.bitcast(sc, jnp.int32)
                key_ref[j, rows, :] = jnp.where(bits < 0, bits ^ np.int32(0x7FFFFFFF), bits)
            return 0

        lax.fori_loop(0, qb + 1, score, 0)

        def count_ge(cand):
            def body(j, part):
                sel = jnp.where(key_ref[j] >= cand, 1.0, 0.0)
                return part + jnp.sum(sel.reshape(blk // 8, 8, blk), axis=0)
            part = lax.fori_loop(0, qb + 1, body, jnp.zeros((8, blk), F32))
            return jnp.sum(part, axis=0, keepdims=True)

        n = float(n_sel)
        zero_i = jnp.zeros((1, blk), jnp.int32)
        ans = jnp.where(count_ge(zero_i) >= n, zero_i, zero_i + INT_MIN)

        def bit_step(t, ans):
            cand = ans | lax.shift_left(jnp.int32(1), 30 - t)
            return jnp.where(count_ge(cand) >= n, cand, ans)

        thr = lax.fori_loop(0, 31, bit_step, ans)

        def make_mask(j, _):
            sel = jnp.where(key_ref[j] >= thr, 0.0, NEG_INF) + jnp.where(j < qb, 0.0, chunk_mask)
            madd_ref[j] = jnp.transpose(sel)
            return 0

        lax.fori_loop(0, qb + 1, make_mask, 0)

    heads = []
    for g in range(DSA_PAIRS):
        q = q_ref[0, :, g * LANES:(g + 1) * LANES]
        zq = jnp.zeros_like(q)
        heads.append((jnp.where(first, q, zq), g, 0))
        heads.append((jnp.where(first, zq, q), g, 1))
    v_refs = (v0_ref, v1_ref)
    m_ref[...] = jnp.full(m_ref.shape, M_INIT, F32)
    acc_ref[...] = jnp.zeros(acc_ref.shape, F32)

    def block(j, near):
        ks = pl.multiple_of(j * blk, blk)
        kb = k_ref[0, pl.ds(ks, blk), :]

        def scores(tile):
            i, rows = tile
            q, g, hh = heads[i]
            s = _dot_nt(q[rows], kb) + madd_ref[j, rows, :]
            if near is None:
                return s + bfar_ref[2 * DSA_PAIRS * hb + 2 * g + hh]
            return s + bias_ref[0, g, hh, near, rows, :]

        def update(tile, s):
            i, rows = tile
            _flash_update(m_ref, acc_ref, i, rows, s, v_refs[heads[i][2]][0, pl.ds(ks, blk), :])

        _lookahead([(i, rows) for i in range(len(heads)) for rows in _row_tiles(blk)], scores, update)

    def far(j, carry):
        block(j, None)
        return carry

    lax.fori_loop(0, jnp.maximum(qb - 1, 0), far, 0)

    @pl.when(qb >= 1)
    def _previous():
        block(qb - 1, 1)

    block(qb, 0)
    for g in range(DSA_PAIRS):
        o = jnp.where(first, _normalized(acc_ref[2 * g], 0), _normalized(acc_ref[2 * g + 1], 1))
        o_ref[0, :, g * LANES:(g + 1) * LANES] = o.astype(BF16)


def _dsa_attn(bfar, qi, ki, wi, q, k, v, bias, n_sel):
    b, s, _ = q.shape
    blk = ATT_BLK
    nk = s // blk
    nhb = H_C // (2 * DSA_PAIRS)
    width = DSA_PAIRS * LANES
    kvp = lambda hb: hb * DSA_PAIRS // C_GROUP
    bias = bias.reshape(nhb, DSA_PAIRS, 2, 2, blk, blk)
    body = functools.partial(_dsa_body, blk=blk, n_sel=n_sel)
    return pl.pallas_call(
        body,
        grid=(b, nk, nhb),
        in_specs=[_smem(),
                  pl.BlockSpec((1, blk, 1024), lambda i, qb, hb: (i, qb, 0)),
                  pl.BlockSpec((1, s, LANES), lambda i, qb, hb: (i, 0, 0)),
                  pl.BlockSpec((1, blk, LANES), lambda i, qb, hb: (i, qb, 0)),
                  pl.BlockSpec((1, blk, width), lambda i, qb, hb: (i, qb, hb)),
                  pl.BlockSpec((1, s, LANES), lambda i, qb, hb: (i, 0, kvp(hb))),
                  pl.BlockSpec((1, s, LANES), lambda i, qb, hb: (i, 0, 2 * kvp(hb))),
                  pl.BlockSpec((1, s, LANES), lambda i, qb, hb: (i, 0, 2 * kvp(hb) + 1)),
                  pl.BlockSpec((1, DSA_PAIRS, 2, 2, blk, blk), lambda i, qb, hb: (hb, 0, 0, 0, 0, 0))],
        out_specs=pl.BlockSpec((1, blk, width), lambda i, qb, hb: (i, qb, hb)),
        out_shape=jax.ShapeDtypeStruct((b, s, H_C * C_DH), BF16),
        scratch_shapes=[pltpu.VMEM((nk, blk, blk), jnp.int32),
                        pltpu.VMEM((nk, blk, blk), F32),
                        pltpu.VMEM((2 * DSA_PAIRS, blk, LANES), F32),
                        pltpu.VMEM((2 * DSA_PAIRS, blk, LANES), F32)],
        compiler_params=_cparams("parallel", "arbitrary", "arbitrary"),
        name="dsa_attn",
    )(bfar, qi, ki, wi, q, k, v, v, bias)


def _top_rows(cur, n):
    vals = []
    for _ in range(n):
        m = jnp.max(cur, axis=0, keepdims=True)
        vals.append(m)
        cur = jnp.where(cur == m, NEG_INF, cur)
    return vals


def _peer_route_body(h_ref, wq_ref, k1_ref, k2_ref, s2_ref, e2_ref, th_ref, e1_ref, q_scr):
    tm = h_ref.shape[0]
    q = _dot(h_ref[...].astype(BF16), wq_ref[...]).astype(BF16)
    for h in range(P_HEADS):
        q_scr[h] = q[:, h * P_DQ:(h + 1) * P_DQ]
    row16 = lax.broadcasted_iota(jnp.int32, (P_TOPK, tm), 0)

    def head(h):
        qh = q_scr[h]
        s1 = _dot_nt(k1_ref[...], qh[:, 0:P_DHALF])
        s2 = _dot_nt(k2_ref[...], qh[:, P_DHALF:P_DQ])
        t1 = _top_rows(s1, P_TOPK + 1)
        t2 = _top_rows(s2, P_TOPK + 1)
        t1m = jnp.zeros((P_TOPK, tm), F32)
        t2m = jnp.zeros((P_TOPK, tm), F32)
        for r in range(P_TOPK):
            t1m = jnp.where(row16 == r, t1[r], t1m)
            t2m = jnp.where(row16 == r, t2[r], t2m)
        cand = jnp.concatenate([t1[r] + t2m for r in range(4)]
                               + [t1[r] + t2m[0:8] for r in range(4, 8)]
                               + [t1m[8:16] + t2[0]], axis=0)
        c = _top_rows(cand, P_TOPK + 1)
        c17 = jnp.maximum(c[P_TOPK], jnp.maximum(t1[P_TOPK] + t2[0], t1[0] + t2[P_TOPK]))
        tau = 0.5 * (c[P_TOPK - 1] + c17)
        z = jnp.sum(jnp.where(cand >= tau, jnp.exp(cand - (t1[0] + t2[0])), 0.0), axis=0, keepdims=True)
        outs = ((s2_ref, s2), (e2_ref, jnp.exp(s2 - t2[0])), (th_ref, tau - s1),
                (e1_ref, jnp.exp(s1 - t1[0]) / z))
        for ref, val in outs:
            for c in range(tm // LANES):
                ref[c, h] = val[:, c * LANES:(c + 1) * LANES]

    def head_pair(hp, carry):
        head(2 * hp)
        head(2 * hp + 1)
        return carry

    lax.fori_loop(0, P_HEADS // 2, head_pair, 0)


def _peer_route(h1, wq, k1, k2):
    t, d = h1.shape
    tm = min(256, t)
    out = pl.BlockSpec((tm // LANES, P_HEADS, N_KEYS, LANES), lambda i: (i, 0, 0, 0))
    return pl.pallas_call(
        _peer_route_body,
        grid=(t // tm,),
        in_specs=[pl.BlockSpec((tm, d), lambda i: (i, 0)), _full(wq.shape), _full(k1.shape), _full(k2.shape)],
        out_specs=[out] * 4,
        out_shape=[jax.ShapeDtypeStruct((t // LANES, P_HEADS, N_KEYS, LANES), F32)] * 4,
        scratch_shapes=[pltpu.VMEM((P_HEADS, tm, P_DQ), BF16)],
        compiler_params=_cparams("parallel"),
        name="peer_route",
    )(h1, wq, k1, k2)


def _peer_main_body(h1_ref, u_ref, vt_ref, s2_ref, e2_ref, th_ref, e1_ref, g_ref, b_ref, out_ref,
                    xt_ref, acc_ref, at_ref, gw_ref):
    e = pl.program_id(1)
    tm = h1_ref.shape[0]
    nsl = PEER_TE // N_KEYS

    @pl.when(e == 0)
    def _init():
        xt_ref[...] = jnp.transpose(h1_ref[...]).astype(BF16)
        acc_ref[...] = jnp.zeros_like(acc_ref)

    i0 = pl.multiple_of(e * nsl, nsl)
    quarter = 2 * N_KEYS

    def up_proj(q):
        rows = slice(q * quarter, (q + 1) * quarter)
        at_ref[rows, :] = _dot(u_ref[rows, :], xt_ref[...])

    def gate(q):
        slabs = (2 * q, 2 * q + 1)
        for c in range(tm // LANES):
            ls = slice(c * LANES, (c + 1) * LANES)
            ws = [jnp.zeros((N_KEYS, LANES), F32) for _ in slabs]
            for h in range(P_HEADS):
                s2t = s2_ref[c, h]
                e2t = e2_ref[c, h]
                thg = th_ref[c, h, pl.ds(i0, nsl), :]
                e1g = e1_ref[c, h, pl.ds(i0, nsl), :]
                for k, ii in enumerate(slabs):
                    ws[k] = ws[k] + jnp.where(s2t >= thg[ii:ii + 1, :], e2t, 0.0) * e1g[ii:ii + 1, :]
            for k, ii in enumerate(slabs):
                rows = slice(ii * N_KEYS, (ii + 1) * N_KEYS)
                gw_ref[rows, ls] = (jax.nn.gelu(at_ref[rows, ls]) * ws[k]).astype(BF16)

    def down_proj(half):
        rows = slice(half * 2 * quarter, (half + 1) * 2 * quarter)
        acc_ref[...] += _dot(vt_ref[:, rows], gw_ref[rows, :])

    up_proj(0)
    up_proj(1)
    for q in range(4):
        gate(q)
        if q % 2 == 1:
            down_proj(q // 2)
        if q + 2 < 4:
            up_proj(q + 2)

    @pl.when(e == pl.num_programs(1) - 1)
    def _finish():
        y = DN_ALPHA * h1_ref[...] + jnp.transpose(acc_ref[...])
        out_ref[...] = _layer_norm(y, g_ref[...], b_ref[...])


def _peer_main(h1, u, vt, s2, e2, th, e1, g, b):
    t, d = h1.shape
    tm = min(512, t)
    route = pl.BlockSpec((tm // LANES, P_HEADS, N_KEYS, LANES), lambda i, e: (i, 0, 0, 0))
    return pl.pallas_call(
        _peer_main_body,
        grid=(t // tm, N_EXPERTS // PEER_TE),
        in_specs=[pl.BlockSpec((tm, d), lambda i, e: (i, 0)),
                  pl.BlockSpec((PEER_TE, d), lambda i, e: (e, 0)),
                  pl.BlockSpec((d, PEER_TE), lambda i, e: (0, e)),
                  route, route, route, route, _full(g.shape), _full(b.shape)],
        out_specs=pl.BlockSpec((tm, d), lambda i, e: (i, 0)),
        out_shape=jax.ShapeDtypeStruct((t, d), F32),
        scratch_shapes=[pltpu.VMEM((d, tm), BF16), pltpu.VMEM((d, tm), F32),
                        pltpu.VMEM((PEER_TE, tm), F32), pltpu.VMEM((PEER_TE, tm), BF16)],
        compiler_params=_cparams("parallel", "arbitrary"),
        name="peer_main",
    )(h1, u, vt, s2, e2, th, e1, g, b)


def _ple_body(h_ref, p_ref, gw_ref, gb_ref, pw_ref, out_ref):
    h = h_ref[...]
    gate = jax.nn.sigmoid(_dot(h.astype(BF16), gw_ref[...]) + gb_ref[...])
    out_ref[...] = h + gate * _dot(p_ref[...].astype(BF16), pw_ref[...])


def _ple(h, p, gw, gb, pw):
    t, d = h.shape
    tm = min(512, t)
    return pl.pallas_call(
        _ple_body,
        grid=(t // tm,),
        in_specs=[pl.BlockSpec((tm, d), lambda i: (i, 0)), pl.BlockSpec((tm, PLE_DIM), lambda i: (i, 0)),
                  _full(gw.shape), _full(gb.shape), _full(pw.shape)],
        out_specs=pl.BlockSpec((tm, d), lambda i: (i, 0)),
        out_shape=jax.ShapeDtypeStruct((t, d), F32),
        compiler_params=_cparams("parallel"),
        name="ple",
    )(h, p, gw, gb, pw)


def _t5_bucket(rel):
    nb = N_BUCKETS // 2
    max_exact = nb // 2
    n = jnp.abs(rel)
    large = max_exact + (jnp.log(jnp.maximum(n, 1).astype(F32) / max_exact)
                         / math.log(MAX_DISTANCE / max_exact) * (nb - max_exact)).astype(jnp.int32)
    large = jnp.minimum(large, nb - 1)
    return jnp.where(rel > 0, nb, 0) + jnp.where(n < max_exact, n, large)


def _bias_tables(tab, heads, blk):
    r = jnp.arange(blk, dtype=jnp.int32)[:, None]
    c = jnp.arange(blk, dtype=jnp.int32)[None, :]
    rel = jnp.stack([c - r, c - r - blk])
    onehot = (_t5_bucket(rel).reshape(1, -1) == jnp.arange(N_BUCKETS, dtype=jnp.int32)[:, None]).astype(F32)
    tiles = jnp.dot(tab[:, jnp.asarray(heads)].T.astype(F32), onehot, precision=lax.Precision.HIGHEST)
    tiles = tiles.reshape(len(heads) // 2, 2, 2, blk, blk)
    far = tab[_t5_bucket(jnp.int32(-(blk + 1)))][jnp.asarray(heads)]
    return tiles.astype(F32) * LOG2E, far.astype(F32) * LOG2E


def _pad_heads(w, n_heads):
    rows = w.shape[0]
    w = w.reshape(rows, n_heads, LANES // 2)
    return jnp.concatenate([w, jnp.zeros_like(w)], axis=-1).reshape(rows, n_heads * LANES)


def _swap_half(w):
    half = w.shape[-1] // 2
    return jnp.concatenate([-w[..., half:], w[..., :half]], axis=-1)


def _even_weights(w_in, w_uq, w_ukv):
    c_q, c_kv, k_r = w_in[:, 0:512], w_in[:, 512:768], w_in[:, 768:800]
    qk_d, v_d = w_in[:, 800:1824], w_in[:, 1824:2336]
    w_in2 = jnp.concatenate([c_q, c_kv, qk_d, _pad_heads(v_d, H_B), jnp.tile(k_r, (1, 4)),
                             jnp.tile(_swap_half(k_r), (1, 4))], axis=1)
    uq = w_uq.reshape(A_Q_RANK, H_A, A_NOPE + A_ROPE)
    rope = uq[:, :, A_NOPE:]
    w_uq2 = jnp.concatenate([uq[:, :, :A_NOPE].reshape(A_Q_RANK, -1), rope.reshape(A_Q_RANK, -1),
                             _swap_half(rope).reshape(A_Q_RANK, -1)], axis=1)
    ukv = w_ukv.reshape(A_KV_RANK, H_A, A_NOPE + A_V)
    w_ukv2 = jnp.concatenate([ukv[:, :, :A_NOPE].reshape(A_KV_RANK, -1),
                              _pad_heads(ukv[:, :, A_NOPE:].reshape(A_KV_RANK, -1), H_A)], axis=1)
    return w_in2.astype(BF16), w_uq2.astype(BF16), w_ukv2.astype(BF16)


_ODD_HEADS = [(2 * pp + a) * C_GROUP + g for pp in range(KV_C // 2) for g in range(C_GROUP) for a in range(2)]


def _odd_weights(w_in, w_o):
    d = w_in.shape[0]
    heads = jnp.asarray(_ODD_HEADS)
    wq = w_in[:, 0:1024].reshape(d, H_C, C_DH)[:, heads].reshape(d, -1)
    wk, wv = w_in[:, 1024:1280], w_in[:, 1280:1536]
    wqi = w_in[:, 1536:2560].reshape(d, H_IDX, IDX_DH)
    wqi_s = jnp.concatenate([_swap_half(wqi[:, :, :IDX_ROPE]), jnp.zeros_like(wqi[:, :, IDX_ROPE:])], axis=-1)
    wki = w_in[:, 2560:2624]
    wki_s = jnp.concatenate([_swap_half(wki[:, :IDX_ROPE]), jnp.zeros_like(wki[:, IDX_ROPE:])], axis=-1)
    wwi = jnp.pad(w_in[:, 2624:2640], ((0, 0), (0, LANES - H_IDX)))
    w2 = jnp.concatenate([wq, wk, _pad_heads(wv, KV_C), wqi.reshape(d, -1), wqi_s.reshape(d, -1),
                          jnp.tile(wki, (1, 2)), jnp.tile(wki_s, (1, 2)), wwi], axis=1)
    w_o2 = w_o.reshape(H_C, C_DH, -1)[heads].reshape(H_C * C_DH, -1)
    return w2.astype(BF16), w_o2.astype(BF16)


def _rope_tables(positions):
    half = A_ROPE // 2
    freqs = ROPE_THETA ** (-jnp.arange(half, dtype=F32) / half)
    ang = positions.astype(F32)[..., None] * freqs
    cos, sin = jnp.cos(ang), jnp.sin(ang)
    cos32 = jnp.concatenate([cos, cos], axis=-1)
    sin32 = jnp.concatenate([sin, sin], axis=-1)
    ones, zeros = jnp.ones_like(cos32), jnp.zeros_like(sin32)
    return (jnp.tile(cos32, (1, 1, 4)), jnp.tile(sin32, (1, 1, 4)),
            jnp.tile(jnp.concatenate([cos32, ones], axis=-1), (1, 1, 2)),
            jnp.tile(jnp.concatenate([sin32, zeros], axis=-1), (1, 1, 2)))


def kernel(x, p, positions, rel_bias, ev_w_in, ev_w_uq, ev_w_ukv, ev_q_norm, ev_kv_norm, ev_lam_q1, ev_lam_k1, ev_lam_q2, ev_lam_k2, ev_subln, ev_w_o, od_w_in, od_w_o, ln1_g, ln1_b, ln2_g, ln2_b, peer_w_q, peer_k1, peer_k2, peer_u, peer_v, ple_w, ple_gate_w, ple_gate_b):
    b, s, d = x.shape
    t = b * s
    assert s % ATT_BLK == 0 and d == D_MODEL
    n_sel = min(TOPK_MAX, s // 4)
    cos_f, sin_f, cos_p, sin_p = _rope_tables(positions)
    bias_b, bfar_b = _bias_tables(rel_bias[:, :H_B], list(range(H_B)), ATT_BLK)
    bias_c, bfar_c = _bias_tables(rel_bias[:, H_B:], _ODD_HEADS, ATT_BLK)
    row = lambda a: a.reshape(1, -1).astype(F32)

    h = x
    for i in range(DEPTH):
        j = i // 2
        if i % 2 == 0:
            lam_init = 0.8 - 0.6 * math.exp(-0.3 * i)
            lam = (jnp.exp(jnp.sum(ev_lam_q1[j] * ev_lam_k1[j], dtype=F32))
                   - jnp.exp(jnp.sum(ev_lam_q2[j] * ev_lam_k2[j], dtype=F32)) + lam_init).reshape(1)
            w_in2, w_uq2, w_ukv2 = _even_weights(ev_w_in[j], ev_w_uq[j], ev_w_ukv[j])
            qn, qr, kn, kr, va, qd, kd, vd = _even_proj(h, cos_f, sin_f, w_in2, w_uq2, w_ukv2,
                                                        row(ev_q_norm[j]), row(ev_kv_norm[j]))
            sg = row(jnp.tile(ev_subln[j], 2))
            oa, od = _even_attn(lam, bfar_b, qn, qr, kn, kr, va, qd, kd, vd, bias_b, sg, lam_init)
            w_o = ev_w_o[j].astype(BF16)
            os_ = [oa.reshape(t, -1), od.reshape(t, -1)]
            ws = [w_o[:H_A * A_V], w_o[H_A * A_V:]]
        else:
            w_in2, w_o2 = _odd_weights(od_w_in[j], od_w_o[j])
            q, k, v, qi, ki, wi = _odd_proj(h, cos_p, sin_p, w_in2)
            o = _dsa_attn(bfar_c, qi, ki, wi, q, k, v, bias_c, n_sel)
            os_, ws = [o.reshape(t, -1)], [w_o2]
        h1 = _oproj_ln(os_, ws, h.reshape(t, d), row(ln1_g[i]), row(ln1_b[i]))
        s2, e2, th, e1 = _peer_route(h1, peer_w_q[i].astype(BF16), peer_k1[i].astype(BF16),
                                     peer_k2[i].astype(BF16))
        h2 = _peer_main(h1, peer_u[i].astype(BF16), jnp.transpose(peer_v[i]).astype(BF16),
                        s2, e2, th, e1, row(ln2_g[i]), row(ln2_b[i]))
        h = _ple(h2, p[i].reshape(t, PLE_DIM), ple_gate_w[i].astype(BF16), row(ple_gate_b[i]),
                 ple_w[i].astype(BF16)).reshape(b, s, d)
    return h
```

```python
import functools
import math

import numpy as np
import jax
import jax.numpy as jnp
from jax import lax
from jax.experimental import pallas as pl
from jax.experimental.pallas import tpu as pltpu

D_MODEL = 1024
DEPTH = 4
CHUNK = 64
PLE_DIM = 256
N_BUCKETS = 32
MAX_DISTANCE = 256
ROPE_THETA = 10000.0
LN_EPS = 1e-5
RMS_EPS = 1e-6
H_A, A_NOPE, A_ROPE, A_V, A_Q_RANK, A_KV_RANK = 8, 64, 32, 64, 512, 256
A_SCALE = (A_NOPE + A_ROPE) ** -0.5
H_B, B_DH = 8, 32
B_SCALE = B_DH ** -0.5
H_C, KV_C, C_DH = 16, 4, 64
C_GROUP = H_C // KV_C
C_SCALE = C_DH ** -0.5
H_IDX, IDX_DH, IDX_ROPE = 16, 64, 32
IDX_SCALE = IDX_DH ** -0.5
TOPK_MAX = 256
P_HEADS, N_KEYS, P_DQ, P_TOPK = 8, 128, 256, 16
P_DHALF = P_DQ // 2
N_EXPERTS = N_KEYS * N_KEYS
DN_ALPHA = (2 * DEPTH) ** 0.25

LANES = 128
ATT_BLK = 256
ATT_ROWS = 256
ATT_AHEAD = 2
DSA_PAIRS = 4
PEER_TE = 8 * N_KEYS
VMEM_LIMIT = 52 * 1024 * 1024

LOG2E = math.log2(math.e)

BF16 = jnp.bfloat16
F32 = jnp.float32
NEG_INF = float("-inf")
M_INIT = -1e30
INT_MIN = np.int32(-2 ** 31)


def _cparams(*sem):
    return pltpu.CompilerParams(dimension_semantics=sem, vmem_limit_bytes=VMEM_LIMIT)


def _full(shape):
    n = len(shape)
    return pl.BlockSpec(shape, lambda *_: (0,) * n)


def _smem():
    return pl.BlockSpec(memory_space=pltpu.SMEM)


def _rms(v, g):
    return v * lax.rsqrt(jnp.mean(v * v, axis=-1, keepdims=True) + RMS_EPS) * g


def _layer_norm(y, g, b):
    mu = jnp.mean(y, axis=-1, keepdims=True)
    yc = y - mu
    var = jnp.mean(yc * yc, axis=-1, keepdims=True)
    return yc * lax.rsqrt(var + LN_EPS) * g + b


def _dot(a, b):
    return jnp.dot(a, b, preferred_element_type=F32)


def _dot_nt(a, b):
    return lax.dot_general(a, b, (((1,), (1,)), ((), ())), preferred_element_type=F32)


def _even_proj_body(h_ref, cos_ref, sin_ref, w_in_ref, w_uq_ref, w_ukv_ref, qg_ref, kvg_ref,
                    qn_ref, qr_ref, kn_ref, kr_ref, va_ref, qd_ref, kd_ref, vd_ref):
    x = h_ref[0].astype(BF16)
    acc = _dot(x, w_in_ref[...])
    cos = cos_ref[0]
    sin = sin_ref[0]
    cq = _rms(acc[:, 0:512], qg_ref[...]).astype(BF16)
    qa = _dot(cq, w_uq_ref[...])
    qn_ref[0] = (qa[:, 0:512] * (A_SCALE * LOG2E)).astype(BF16)
    cos2 = jnp.concatenate([cos, cos], axis=1)
    sin2 = jnp.concatenate([sin, sin], axis=1)
    qr_ref[0] = ((qa[:, 512:768] * cos2 + qa[:, 768:1024] * sin2) * (A_SCALE * LOG2E)).astype(BF16)
    ckv = _rms(acc[:, 512:768], kvg_ref[...]).astype(BF16)
    kv = _dot(ckv, w_ukv_ref[...])
    ones_hi = (lax.broadcasted_iota(jnp.int32, (1, 1024), 1) % LANES >= LANES // 2).astype(F32)
    kn_ref[0] = kv[:, 0:512].astype(BF16)
    va_ref[0] = (kv[:, 512:1536] + ones_hi).astype(BF16)
    qd_ref[0] = (acc[:, 768:1280] * (B_SCALE * LOG2E)).astype(BF16)
    kd_ref[0] = acc[:, 1280:1792].astype(BF16)
    vd_ref[0] = (acc[:, 1792:2816] + ones_hi).astype(BF16)
    kr_ref[0] = (acc[:, 2816:2944] * cos + acc[:, 2944:3072] * sin).astype(BF16)


def _even_proj(h, cos, sin, w_in, w_uq, w_ukv, qg, kvg):
    b, s, d = h.shape
    tm = min(512, s)
    tok = lambda w: pl.BlockSpec((1, tm, w), lambda i, j: (i, j, 0))
    widths = (512, 256, 512, 128, 1024, 512, 512, 1024)
    return pl.pallas_call(
        _even_proj_body,
        grid=(b, s // tm),
        in_specs=[tok(d), tok(LANES), tok(LANES), _full(w_in.shape), _full(w_uq.shape),
                  _full(w_ukv.shape), _full(qg.shape), _full(kvg.shape)],
        out_specs=[tok(w) for w in widths],
        out_shape=[jax.ShapeDtypeStruct((b, s, w), BF16) for w in widths],
        compiler_params=_cparams("parallel", "parallel"),
        name="even_proj",
    )(h, cos, sin, w_in, w_uq, w_ukv, qg, kvg)


def _flash_update(m_ref, acc_ref, i, rows, s, v):
    m_prev = m_ref[i, rows, :]
    m_new = jnp.maximum(m_prev, jnp.max(s, axis=1, keepdims=True))
    alpha = jnp.exp2(m_prev - m_new)
    p = jnp.exp2(s - jnp.concatenate([m_new] * (s.shape[1] // LANES), axis=1))
    acc_ref[i, rows, :] = alpha * acc_ref[i, rows, :] + _dot(p.astype(BF16), v)
    m_ref[i, rows, :] = m_new


def _row_tiles(blk):
    return [slice(r, r + ATT_ROWS) for r in range(0, blk, ATT_ROWS)]


def _lookahead(tiles, produce, consume):
    pending = [produce(t) for t in tiles[:ATT_AHEAD]]
    for n, t in enumerate(tiles):
        if n + ATT_AHEAD < len(tiles):
            pending.append(produce(tiles[n + ATT_AHEAD]))
        consume(t, pending.pop(0))


def _normalized(acc, hh):
    r = pltpu.roll(acc, LANES // 2, 1)
    return acc / r if hh == 0 else r / acc


def _chunk_allowed(tq, tk):
    r = lax.broadcasted_iota(jnp.int32, (tq, tk), 0) // CHUNK
    c = lax.broadcasted_iota(jnp.int32, (tq, tk), 1) // CHUNK
    return c <= r


def _even_attn_body(lam_ref, bfar_ref, qn_ref, qr_ref, kn_ref, kr_ref, va_ref, qd_ref, kd_ref, vd_ref,
                    bias_ref, sg_ref, oa_ref, od_ref, m_ref, acc_ref, *, blk, lam_init):
    p = pl.program_id(1)
    qi = pl.program_id(2)
    lane = lax.broadcasted_iota(jnp.int32, (blk, LANES), 1)
    qn = qn_ref[0]
    qr = qr_ref[0]
    qd = qd_ref[0]
    zero = jnp.zeros_like(qn)
    streams = []
    for hh in range(2):
        lo = 64 * hh
        r0 = 32 * (2 * (p % 2) + hh)
        qn_m = jnp.where((lane >= lo) & (lane < lo + 64), qn, zero)
        qr_m = jnp.where((lane >= r0) & (lane < r0 + 32), qr, zero)
        streams.append((jnp.concatenate([qn_m, qr_m], axis=1), "a", hh))
        streams.append((jnp.where((lane >= lo) & (lane < lo + 32), qd, zero), "d", hh))
        streams.append((jnp.where((lane >= lo + 32) & (lane < lo + 64), qd, zero), "d", hh))
    allowed = _chunk_allowed(blk, blk)

    m_ref[...] = jnp.full(m_ref.shape, M_INIT, F32)
    acc_ref[...] = jnp.zeros(acc_ref.shape, F32)

    def block(j, near):
        ks = pl.multiple_of(j * blk, blk)
        ka = jnp.concatenate([kn_ref[0, pl.ds(ks, blk), :], kr_ref[0, pl.ds(ks, blk), :]], axis=1)
        kd = kd_ref[0, pl.ds(ks, blk), :]
        def scores(tile):
            i, rows = tile
            q, kind, hh = streams[i]
            if kind == "a":
                s = _dot_nt(q[rows], ka)
            else:
                s = _dot_nt(q[rows], kd)
                s = s + (bfar_ref[2 * p + hh] if near is None else bias_ref[0, hh, near, rows, :])
            if near == 0:
                s = jnp.where(allowed[rows], s, NEG_INF)
            return s

        def update(tile, s):
            i, rows = tile
            _, kind, hh = streams[i]
            v_ref = va_ref if kind == "a" else vd_ref
            _flash_update(m_ref, acc_ref, i, rows, s, v_ref[0, pl.ds(ks, blk), hh * LANES:(hh + 1) * LANES])

        _lookahead([(i, rows) for i in range(len(streams)) for rows in _row_tiles(blk)], scores, update)

    def far(j, carry):
        block(j, None)
        return carry

    lax.fori_loop(0, jnp.maximum(qi - 1, 0), far, 0)

    @pl.when(qi >= 1)
    def _previous():
        block(qi - 1, 1)

    block(qi, 0)

    lam = lam_ref[0]
    first = lane < 64
    o = [_normalized(acc_ref[i], hh) for i, (_, _, hh) in enumerate(streams)]
    oa_ref[0] = jnp.where(first, o[0], o[3]).astype(BF16)
    od = jnp.where(first, o[1] - lam * o[2], o[4] - lam * o[5])
    sq = od * od
    ms0 = jnp.sum(jnp.where(first, sq, 0.0), axis=1, keepdims=True)
    ms1 = jnp.sum(jnp.where(first, 0.0, sq), axis=1, keepdims=True)
    ms = jnp.where(first, ms0, ms1) * (1.0 / (2 * B_DH))
    od_ref[0] = (od * lax.rsqrt(ms + RMS_EPS) * sg_ref[...] * (1.0 - lam_init)).astype(BF16)


def _even_attn(lam, bfar, qn, qr, kn, kr, va, qd, kd, vd, bias, sg, lam_init):
    b, s, _ = qn.shape
    blk = ATT_BLK
    qblk = lambda f: pl.BlockSpec((1, blk, LANES), f)
    kv = lambda f: pl.BlockSpec((1, s, LANES), f)
    vv = lambda f: pl.BlockSpec((1, s, 2 * LANES), f)
    body = functools.partial(_even_attn_body, blk=blk, lam_init=lam_init)
    return pl.pallas_call(
        body,
        grid=(b, H_A // 2, s // blk),
        in_specs=[_smem(), _smem(),
                  qblk(lambda i, p, q: (i, q, p)), qblk(lambda i, p, q: (i, q, p // 2)),
                  kv(lambda i, p, q: (i, 0, p)), kv(lambda i, p, q: (i, 0, 0)), vv(lambda i, p, q: (i, 0, p)),
                  qblk(lambda i, p, q: (i, q, p)), kv(lambda i, p, q: (i, 0, p)), vv(lambda i, p, q: (i, 0, p)),
                  pl.BlockSpec((1, 2, 2, blk, blk), lambda i, p, q: (p, 0, 0, 0, 0)),
                  _full(sg.shape)],
        out_specs=[qblk(lambda i, p, q: (i, q, p)), qblk(lambda i, p, q: (i, q, p))],
        out_shape=[jax.ShapeDtypeStruct((b, s, 512), BF16)] * 2,
        scratch_shapes=[pltpu.VMEM((6, blk, LANES), F32), pltpu.VMEM((6, blk, LANES), F32)],
        compiler_params=_cparams("parallel", "parallel", "arbitrary"),
        name="even_attn",
    )(lam, bfar, qn, qr, kn, kr, va, qd, kd, vd, bias, sg)


def _oproj_ln_body(*refs, n_in):
    o_refs = refs[:n_in]
    w_refs = refs[n_in:2 * n_in]
    h_ref, g_ref, b_ref, out_ref = refs[2 * n_in:]
    mix = _dot(o_refs[0][...], w_refs[0][...])
    for o_ref, w_ref in zip(o_refs[1:], w_refs[1:]):
        mix = mix + _dot(o_ref[...], w_ref[...])
    out_ref[...] = _layer_norm(DN_ALPHA * h_ref[...] + mix, g_ref[...], b_ref[...])


def _oproj_ln(os_, ws, h, g, b):
    t, d = h.shape
    tm = min(512, t)
    tok = lambda w: pl.BlockSpec((tm, w), lambda i: (i, 0))
    return pl.pallas_call(
        functools.partial(_oproj_ln_body, n_in=len(os_)),
        grid=(t // tm,),
        in_specs=[tok(o.shape[1]) for o in os_] + [_full(w.shape) for w in ws]
                 + [tok(d), _full(g.shape), _full(b.shape)],
        out_specs=tok(d),
        out_shape=jax.ShapeDtypeStruct((t, d), F32),
        compiler_params=_cparams("parallel"),
        name="oproj_ln",
    )(*os_, *ws, h, g, b)


def _odd_proj_body(h_ref, cos_ref, sin_ref, w_ref, q_ref, k_ref, v_ref, qi_ref, ki_ref, wi_ref):
    x = h_ref[0].astype(BF16)
    acc = _dot(x, w_ref[...])
    cos = cos_ref[0]
    sin = sin_ref[0]
    q_ref[0] = (acc[:, 0:1024] * (C_SCALE * LOG2E)).astype(BF16)
    k_ref[0] = acc[:, 1024:1280].astype(BF16)
    ones_hi = (lax.broadcasted_iota(jnp.int32, (1, 512), 1) % LANES >= LANES // 2).astype(F32)
    v_ref[0] = (acc[:, 1280:1792] + ones_hi).astype(BF16)
    cos8 = jnp.concatenate([cos] * 8, axis=1)
    sin8 = jnp.concatenate([sin] * 8, axis=1)
    qi_ref[0] = ((acc[:, 1792:2816] * cos8 + acc[:, 2816:3840] * sin8) * IDX_SCALE).astype(BF16)
    ki_ref[0] = (acc[:, 3840:3968] * cos + acc[:, 3968:4096] * sin).astype(BF16)
    wi_ref[0] = acc[:, 4096:4224] * (H_IDX ** -0.5)


def _odd_proj(h, cos, sin, w):
    b, s, d = h.shape
    tm = min(256, s)
    tok = lambda wd: pl.BlockSpec((1, tm, wd), lambda i, j: (i, j, 0))
    widths = (1024, 256, 512, 1024, 128, 128)
    dts = (BF16, BF16, BF16, BF16, BF16, F32)
    return pl.pallas_call(
        _odd_proj_body,
        grid=(b, s // tm),
        in_specs=[tok(d), tok(LANES), tok(LANES), _full(w.shape)],
        out_specs=[tok(wd) for wd in widths],
        out_shape=[jax.ShapeDtypeStruct((b, s, wd), dt) for wd, dt in zip(widths, dts)],
        compiler_params=_cparams("parallel", "parallel"),
        name="odd_proj",
    )(h, cos, sin, w)


def _dsa_body(bfar_ref, qi_ref, ki_ref, wi_ref, q_ref, k_ref, v0_ref, v1_ref, bias_ref, o_ref,
              key_ref, madd_ref, m_ref, acc_ref, *, blk, n_sel):
    qb = pl.program_id(1)
    hb = pl.program_id(2)
    lane = lax.broadcasted_iota(jnp.int32, (blk, LANES), 1)
    first = lane < 64

    @pl.when(hb == 0)
    def _select():
        wt = jnp.transpose(wi_ref[0])
        kc = lax.broadcasted_iota(jnp.int32, (blk, blk), 0) // CHUNK
        qc = lax.broadcasted_iota(jnp.int32, (blk, blk), 1) // CHUNK
        chunk_mask = jnp.where(kc <= qc, 0.0, NEG_INF)
        qis = []
        for hp in range(H_IDX // 2):
            qp = qi_ref[0, :, hp * LANES:(hp + 1) * LANES]
            qis.append(jnp.where(first, qp, jnp.zeros_like(qp)))
            qis.append(jnp.where(first, jnp.zeros_like(qp), qp))

        def score(j, _):
            ks = pl.multiple_of(j * blk, blk)
            for rows in (slice(0, blk // 2), slice(blk // 2, blk)):
                kib = ki_ref[0, pl.ds(ks + rows.start, blk // 2), :]
                sc = jnp.zeros((blk // 2, blk), F32)
                for h in range(H_IDX):
                    sc = sc + jnp.maximum(_dot_nt(kib, qis[h]), 0.0) * wt[h:h + 1, :]
                sc = sc + jnp.where(j < qb, 0.0, chunk_mask[rows])
                bits = pltpu.bitcast(sc, jnp.int32)
                key_ref[j, rows, :] = jnp.where(bits < 0, bits ^ np.int32(0x7FFFFFFF), bits)
            return 0

        lax.fori_loop(0, qb + 1, score, 0)

        def count_ge(cand):
            def body(j, part):
                sel = jnp.where(key_ref[j] >= cand, 1.0, 0.0)
                return part + jnp.sum(sel.reshape(blk // 8, 8, blk), axis=0)
            part = lax.fori_loop(0, qb + 1, body, jnp.zeros((8, blk), F32))
            return jnp.sum(part, axis=0, keepdims=True)

        n = float(n_sel)
        zero_i = jnp.zeros((1, blk), jnp.int32)
        ans = jnp.where(count_ge(zero_i) >= n, zero_i, zero_i + INT_MIN)

        def bit_step(t, ans):
            cand = ans | lax.shift_left(jnp.int32(1), 30 - t)
            return jnp.where(count_ge(cand) >= n, cand, ans)

        thr = lax.fori_loop(0, 31, bit_step, ans)

        def make_mask(j, _):
            sel = jnp.where(key_ref[j] >= thr, 0.0, NEG_INF) + jnp.where(j < qb, 0.0, chunk_mask)
            madd_ref[j] = jnp.transpose(sel)
            return 0

        lax.fori_loop(0, qb + 1, make_mask, 0)

    heads = []
    for g in range(DSA_PAIRS):
        q = q_ref[0, :, g * LANES:(g + 1) * LANES]
        zq = jnp.zeros_like(q)
        heads.append((jnp.where(first, q, zq), g, 0))
        heads.append((jnp.where(first, zq, q), g, 1))
    v_refs = (v0_ref, v1_ref)
    m_ref[...] = jnp.full(m_ref.shape, M_INIT, F32)
    acc_ref[...] = jnp.zeros(acc_ref.shape, F32)

    def block(j, near):
        ks = pl.multiple_of(j * blk, blk)
        kb = k_ref[0, pl.ds(ks, blk), :]

        def scores(tile):
            i, rows = tile
            q, g, hh = heads[i]
            s = _dot_nt(q[rows], kb) + madd_ref[j, rows, :]
            if near is None:
                return s + bfar_ref[2 * DSA_PAIRS * hb + 2 * g + hh]
            return s + bias_ref[0, g, hh, near, rows, :]

        def update(tile, s):
            i, rows = tile
            _flash_update(m_ref, acc_ref, i, rows, s, v_refs[heads[i][2]][0, pl.ds(ks, blk), :])

        _lookahead([(i, rows) for i in range(len(heads)) for rows in _row_tiles(blk)], scores, update)

    def far(j, carry):
        block(j, None)
        return carry

    lax.fori_loop(0, jnp.maximum(qb - 1, 0), far, 0)

    @pl.when(qb >= 1)
    def _previous():
        block(qb - 1, 1)

    block(qb, 0)
    for g in range(DSA_PAIRS):
        o = jnp.where(first, _normalized(acc_ref[2 * g], 0), _normalized(acc_ref[2 * g + 1], 1))
        o_ref[0, :, g * LANES:(g + 1) * LANES] = o.astype(BF16)


def _dsa_attn(bfar, qi, ki, wi, q, k, v, bias, n_sel):
    b, s, _ = q.shape
    blk = ATT_BLK
    nk = s // blk
    nhb = H_C // (2 * DSA_PAIRS)
    width = DSA_PAIRS * LANES
    kvp = lambda hb: hb * DSA_PAIRS // C_GROUP
    bias = bias.reshape(nhb, DSA_PAIRS, 2, 2, blk, blk)
    body = functools.partial(_dsa_body, blk=blk, n_sel=n_sel)
    return pl.pallas_call(
        body,
        grid=(b, nk, nhb),
        in_specs=[_smem(),
                  pl.BlockSpec((1, blk, 1024), lambda i, qb, hb: (i, qb, 0)),
                  pl.BlockSpec((1, s, LANES), lambda i, qb, hb: (i, 0, 0)),
                  pl.BlockSpec((1, blk, LANES), lambda i, qb, hb: (i, qb, 0)),
                  pl.BlockSpec((1, blk, width), lambda i, qb, hb: (i, qb, hb)),
                  pl.BlockSpec((1, s, LANES), lambda i, qb, hb: (i, 0, kvp(hb))),
                  pl.BlockSpec((1, s, LANES), lambda i, qb, hb: (i, 0, 2 * kvp(hb))),
                  pl.BlockSpec((1, s, LANES), lambda i, qb, hb: (i, 0, 2 * kvp(hb) + 1)),
                  pl.BlockSpec((1, DSA_PAIRS, 2, 2, blk, blk), lambda i, qb, hb: (hb, 0, 0, 0, 0, 0))],
        out_specs=pl.BlockSpec((1, blk, width), lambda i, qb, hb: (i, qb, hb)),
        out_shape=jax.ShapeDtypeStruct((b, s, H_C * C_DH), BF16),
        scratch_shapes=[pltpu.VMEM((nk, blk, blk), jnp.int32),
                        pltpu.VMEM((nk, blk, blk), F32),
                        pltpu.VMEM((2 * DSA_PAIRS, blk, LANES), F32),
                        pltpu.VMEM((2 * DSA_PAIRS, blk, LANES), F32)],
        compiler_params=_cparams("parallel", "arbitrary", "arbitrary"),
        name="dsa_attn",
    )(bfar, qi, ki, wi, q, k, v, v, bias)


def _top_rows(cur, n):
    vals = []
    for _ in range(n):
        m = jnp.max(cur, axis=0, keepdims=True)
        vals.append(m)
        cur = jnp.where(cur == m, NEG_INF, cur)
    return vals


def _peer_route_body(h_ref, wq_ref, k1_ref, k2_ref, s2_ref, e2_ref, th_ref, e1_ref, q_scr):
    tm = h_ref.shape[0]
    q = _dot(h_ref[...].astype(BF16), wq_ref[...]).astype(BF16)
    for h in range(P_HEADS):
        q_scr[h] = q[:, h * P_DQ:(h + 1) * P_DQ]
    row16 = lax.broadcasted_iota(jnp.int32, (P_TOPK, tm), 0)

    def head(h):
        qh = q_scr[h]
        s1 = _dot_nt(k1_ref[...], qh[:, 0:P_DHALF])
        s2 = _dot_nt(k2_ref[...], qh[:, P_DHALF:P_DQ])
        t1 = _top_rows(s1, P_TOPK + 1)
        t2 = _top_rows(s2, P_TOPK + 1)
        t1m = jnp.zeros((P_TOPK, tm), F32)
        t2m = jnp.zeros((P_TOPK, tm), F32)
        for r in range(P_TOPK):
            t1m = jnp.where(row16 == r, t1[r], t1m)
            t2m = jnp.where(row16 == r, t2[r], t2m)
        cand = jnp.concatenate([t1[r] + t2m for r in range(4)]
                               + [t1[r] + t2m[0:8] for r in range(4, 8)]
                               + [t1m[8:16] + t2[0]], axis=0)
        c = _top_rows(cand, P_TOPK + 1)
        c17 = jnp.maximum(c[P_TOPK], jnp.maximum(t1[P_TOPK] + t2[0], t1[0] + t2[P_TOPK]))
        tau = 0.5 * (c[P_TOPK - 1] + c17)
        z = jnp.sum(jnp.where(cand >= tau, jnp.exp(cand - (t1[0] + t2[0])), 0.0), axis=0, keepdims=True)
        outs = ((s2_ref, s2), (e2_ref, jnp.exp(s2 - t2[0])), (th_ref, tau - s1),
                (e1_ref, jnp.exp(s1 - t1[0]) / z))
        for ref, val in outs:
            ref[h] = val

    def head_pair(hp, carry):
        head(2 * hp)
        head(2 * hp + 1)
        return carry

    lax.fori_loop(0, P_HEADS // 2, head_pair, 0)


def _peer_route(h1, wq, k1, k2):
    t, d = h1.shape
    tm = min(256, t)
    out = pl.BlockSpec((P_HEADS, N_KEYS, tm), lambda i: (0, 0, i))
    return pl.pallas_call(
        _peer_route_body,
        grid=(t // tm,),
        in_specs=[pl.BlockSpec((tm, d), lambda i: (i, 0)), _full(wq.shape), _full(k1.shape), _full(k2.shape)],
        out_specs=[out] * 4,
        out_shape=[jax.ShapeDtypeStruct((P_HEADS, N_KEYS, t), F32)] * 4,
        scratch_shapes=[pltpu.VMEM((P_HEADS, tm, P_DQ), BF16)],
        compiler_params=_cparams("parallel"),
        name="peer_route",
    )(h1, wq, k1, k2)


def _peer_main_body(h1_ref, u_ref, vt_ref, s2_ref, e2_ref, th_ref, e1_ref, g_ref, b_ref, out_ref,
                    xt_ref, acc_ref, at_ref, gw_ref):
    e = pl.program_id(1)
    tm = h1_ref.shape[0]
    nsl = PEER_TE // N_KEYS

    @pl.when(e == 0)
    def _init():
        xt_ref[...] = jnp.transpose(h1_ref[...]).astype(BF16)
        acc_ref[...] = jnp.zeros_like(acc_ref)

    i0 = pl.multiple_of(e * nsl, nsl)
    quarter = 2 * N_KEYS

    def up_proj(q):
        rows = slice(q * quarter, (q + 1) * quarter)
        at_ref[rows, :] = _dot(u_ref[rows, :], xt_ref[...])

    def gate(q):
        slabs = (2 * q, 2 * q + 1)
        for c in range(tm // LANES):
            ls = slice(c * LANES, (c + 1) * LANES)
            ws = [jnp.zeros((N_KEYS, LANES), F32) for _ in slabs]
            for h in range(P_HEADS):
                s2t = s2_ref[h, :, ls]
                e2t = e2_ref[h, :, ls]
                thg = th_ref[h, pl.ds(i0, nsl), ls]
                e1g = e1_ref[h, pl.ds(i0, nsl), ls]
                for k, ii in enumerate(slabs):
                    ws[k] = ws[k] + jnp.where(s2t >= thg[ii:ii + 1, :], e2t, 0.0) * e1g[ii:ii + 1, :]
            for k, ii in enumerate(slabs):
                rows = slice(ii * N_KEYS, (ii + 1) * N_KEYS)
                gw_ref[rows, ls] = (jax.nn.gelu(at_ref[rows, ls]) * ws[k]).astype(BF16)

    half = PEER_TE // 2
    partial = []
    up_proj(0)
    for q in range(4):
        if q + 1 < 4:
            up_proj(q + 1)
        gate(q)
        if q % 2 == 1:
            ks = slice((q // 2) * half, (q // 2 + 1) * half)
            partial.append(_dot(vt_ref[:, ks], gw_ref[ks, :]))
    acc_ref[...] += partial[0] + partial[1]

    @pl.when(e == pl.num_programs(1) - 1)
    def _finish():
        y = DN_ALPHA * h1_ref[...] + jnp.transpose(acc_ref[...])
        out_ref[...] = _layer_norm(y, g_ref[...], b_ref[...])


def _peer_main(h1, u, vt, s2, e2, th, e1, g, b):
    t, d = h1.shape
    tm = min(512, t)
    route = pl.BlockSpec((P_HEADS, N_KEYS, tm), lambda i, e: (0, 0, i))
    return pl.pallas_call(
        _peer_main_body,
        grid=(t // tm, N_EXPERTS // PEER_TE),
        in_specs=[pl.BlockSpec((tm, d), lambda i, e: (i, 0)),
                  pl.BlockSpec((PEER_TE, d), lambda i, e: (e, 0)),
                  pl.BlockSpec((d, PEER_TE), lambda i, e: (0, e)),
                  route, route, route, route, _full(g.shape), _full(b.shape)],
        out_specs=pl.BlockSpec((tm, d), lambda i, e: (i, 0)),
        out_shape=jax.ShapeDtypeStruct((t, d), F32),
        scratch_shapes=[pltpu.VMEM((d, tm), BF16), pltpu.VMEM((d, tm), F32),
                        pltpu.VMEM((PEER_TE, tm), F32), pltpu.VMEM((PEER_TE, tm), BF16)],
        compiler_params=_cparams("parallel", "arbitrary"),
        name="peer_main",
    )(h1, u, vt, s2, e2, th, e1, g, b)


def _ple_body(h_ref, p_ref, gw_ref, gb_ref, pw_ref, out_ref):
    h = h_ref[...]
    gate = jax.nn.sigmoid(_dot(h.astype(BF16), gw_ref[...]) + gb_ref[...])
    out_ref[...] = h + gate * _dot(p_ref[...].astype(BF16), pw_ref[...])


def _ple(h, p, gw, gb, pw):
    t, d = h.shape
    tm = min(512, t)
    return pl.pallas_call(
        _ple_body,
        grid=(t // tm,),
        in_specs=[pl.BlockSpec((tm, d), lambda i: (i, 0)), pl.BlockSpec((tm, PLE_DIM), lambda i: (i, 0)),
                  _full(gw.shape), _full(gb.shape), _full(pw.shape)],
        out_specs=pl.BlockSpec((tm, d), lambda i: (i, 0)),
        out_shape=jax.ShapeDtypeStruct((t, d), F32),
        compiler_params=_cparams("parallel"),
        name="ple",
    )(h, p, gw, gb, pw)


def _t5_bucket(rel):
    nb = N_BUCKETS // 2
    max_exact = nb // 2
    n = jnp.abs(rel)
    large = max_exact + (jnp.log(jnp.maximum(n, 1).astype(F32) / max_exact)
                         / math.log(MAX_DISTANCE / max_exact) * (nb - max_exact)).astype(jnp.int32)
    large = jnp.minimum(large, nb - 1)
    return jnp.where(rel > 0, nb, 0) + jnp.where(n < max_exact, n, large)


def _bias_tables(tab, heads, blk):
    r = jnp.arange(blk, dtype=jnp.int32)[:, None]
    c = jnp.arange(blk, dtype=jnp.int32)[None, :]
    rel = jnp.stack([c - r, c - r - blk])
    onehot = (_t5_bucket(rel).reshape(1, -1) == jnp.arange(N_BUCKETS, dtype=jnp.int32)[:, None]).astype(F32)
    tiles = jnp.dot(tab[:, jnp.asarray(heads)].T.astype(F32), onehot, precision=lax.Precision.HIGHEST)
    tiles = tiles.reshape(len(heads) // 2, 2, 2, blk, blk)
    far = tab[_t5_bucket(jnp.int32(-(blk + 1)))][jnp.asarray(heads)]
    return tiles.astype(F32) * LOG2E, far.astype(F32) * LOG2E


def _pad_heads(w, n_heads):
    rows = w.shape[0]
    w = w.reshape(rows, n_heads, LANES // 2)
    return jnp.concatenate([w, jnp.zeros_like(w)], axis=-1).reshape(rows, n_heads * LANES)


def _swap_half(w):
    half = w.shape[-1] // 2
    return jnp.concatenate([-w[..., half:], w[..., :half]], axis=-1)


def _even_weights(w_in, w_uq, w_ukv):
    c_q, c_kv, k_r = w_in[:, 0:512], w_in[:, 512:768], w_in[:, 768:800]
    qk_d, v_d = w_in[:, 800:1824], w_in[:, 1824:2336]
    w_in2 = jnp.concatenate([c_q, c_kv, qk_d, _pad_heads(v_d, H_B), jnp.tile(k_r, (1, 4)),
                             jnp.tile(_swap_half(k_r), (1, 4))], axis=1)
    uq = w_uq.reshape(A_Q_RANK, H_A, A_NOPE + A_ROPE)
    rope = uq[:, :, A_NOPE:]
    w_uq2 = jnp.concatenate([uq[:, :, :A_NOPE].reshape(A_Q_RANK, -1), rope.reshape(A_Q_RANK, -1),
                             _swap_half(rope).reshape(A_Q_RANK, -1)], axis=1)
    ukv = w_ukv.reshape(A_KV_RANK, H_A, A_NOPE + A_V)
    w_ukv2 = jnp.concatenate([ukv[:, :, :A_NOPE].reshape(A_KV_RANK, -1),
                              _pad_heads(ukv[:, :, A_NOPE:].reshape(A_KV_RANK, -1), H_A)], axis=1)
    return w_in2.astype(BF16), w_uq2.astype(BF16), w_ukv2.astype(BF16)


_ODD_HEADS = [(2 * pp + a) * C_GROUP + g for pp in range(KV_C // 2) for g in range(C_GROUP) for a in range(2)]


def _odd_weights(w_in, w_o):
    d = w_in.shape[0]
    heads = jnp.asarray(_ODD_HEADS)
    wq = w_in[:, 0:1024].reshape(d, H_C, C_DH)[:, heads].reshape(d, -1)
    wk, wv = w_in[:, 1024:1280], w_in[:, 1280:1536]
    wqi = w_in[:, 1536:2560].reshape(d, H_IDX, IDX_DH)
    wqi_s = jnp.concatenate([_swap_half(wqi[:, :, :IDX_ROPE]), jnp.zeros_like(wqi[:, :, IDX_ROPE:])], axis=-1)
    wki = w_in[:, 2560:2624]
    wki_s = jnp.concatenate([_swap_half(wki[:, :IDX_ROPE]), jnp.zeros_like(wki[:, IDX_ROPE:])], axis=-1)
    wwi = jnp.pad(w_in[:, 2624:2640], ((0, 0), (0, LANES - H_IDX)))
    w2 = jnp.concatenate([wq, wk, _pad_heads(wv, KV_C), wqi.reshape(d, -1), wqi_s.reshape(d, -1),
                          jnp.tile(wki, (1, 2)), jnp.tile(wki_s, (1, 2)), wwi], axis=1)
    w_o2 = w_o.reshape(H_C, C_DH, -1)[heads].reshape(H_C * C_DH, -1)
    return w2.astype(BF16), w_o2.astype(BF16)


def _rope_tables(positions):
    half = A_ROPE // 2
    freqs = ROPE_THETA ** (-jnp.arange(half, dtype=F32) / half)
    ang = positions.astype(F32)[..., None] * freqs
    cos, sin = jnp.cos(ang), jnp.sin(ang)
    cos32 = jnp.concatenate([cos, cos], axis=-1)
    sin32 = jnp.concatenate([sin, sin], axis=-1)
    ones, zeros = jnp.ones_like(cos32), jnp.zeros_like(sin32)
    return (jnp.tile(cos32, (1, 1, 4)), jnp.tile(sin32, (1, 1, 4)),
            jnp.tile(jnp.concatenate([cos32, ones], axis=-1), (1, 1, 2)),
            jnp.tile(jnp.concatenate([sin32, zeros], axis=-1), (1, 1, 2)))


def kernel(x, p, positions, rel_bias, ev_w_in, ev_w_uq, ev_w_ukv, ev_q_norm, ev_kv_norm, ev_lam_q1, ev_lam_k1, ev_lam_q2, ev_lam_k2, ev_subln, ev_w_o, od_w_in, od_w_o, ln1_g, ln1_b, ln2_g, ln2_b, peer_w_q, peer_k1, peer_k2, peer_u, peer_v, ple_w, ple_gate_w, ple_gate_b):
    b, s, d = x.shape
    t = b * s
    assert s % ATT_BLK == 0 and d == D_MODEL
    n_sel = min(TOPK_MAX, s // 4)
    cos_f, sin_f, cos_p, sin_p = _rope_tables(positions)
    bias_b, bfar_b = _bias_tables(rel_bias[:, :H_B], list(range(H_B)), ATT_BLK)
    bias_c, bfar_c = _bias_tables(rel_bias[:, H_B:], _ODD_HEADS, ATT_BLK)
    row = lambda a: a.reshape(1, -1).astype(F32)

    h = x
    for i in range(DEPTH):
        j = i // 2
        if i % 2 == 0:
            lam_init = 0.8 - 0.6 * math.exp(-0.3 * i)
            lam = (jnp.exp(jnp.sum(ev_lam_q1[j] * ev_lam_k1[j], dtype=F32))
                   - jnp.exp(jnp.sum(ev_lam_q2[j] * ev_lam_k2[j], dtype=F32)) + lam_init).reshape(1)
            w_in2, w_uq2, w_ukv2 = _even_weights(ev_w_in[j], ev_w_uq[j], ev_w_ukv[j])
            qn, qr, kn, kr, va, qd, kd, vd = _even_proj(h, cos_f, sin_f, w_in2, w_uq2, w_ukv2,
                                                        row(ev_q_norm[j]), row(ev_kv_norm[j]))
            sg = row(jnp.tile(ev_subln[j], 2))
            oa, od = _even_attn(lam, bfar_b, qn, qr, kn, kr, va, qd, kd, vd, bias_b, sg, lam_init)
            w_o = ev_w_o[j].astype(BF16)
            os_ = [oa.reshape(t, -1), od.reshape(t, -1)]
            ws = [w_o[:H_A * A_V], w_o[H_A * A_V:]]
        else:
            w_in2, w_o2 = _odd_weights(od_w_in[j], od_w_o[j])
            q, k, v, qi, ki, wi = _odd_proj(h, cos_p, sin_p, w_in2)
            o = _dsa_attn(bfar_c, qi, ki, wi, q, k, v, bias_c, n_sel)
            os_, ws = [o.reshape(t, -1)], [w_o2]
        h1 = _oproj_ln(os_, ws, h.reshape(t, d), row(ln1_g[i]), row(ln1_b[i]))
        s2, e2, th, e1 = _peer_route(h1, peer_w_q[i].astype(BF16), peer_k1[i].astype(BF16),
                                     peer_k2[i].astype(BF16))
        h2 = _peer_main(h1, peer_u[i].astype(BF16), jnp.transpose(peer_v[i]).astype(BF16),
                        s2, e2, th, e1, row(ln2_g[i]), row(ln2_b[i]))
        h = _ple(h2, p[i].reshape(t, PLE_DIM), ple_gate_w[i].astype(BF16), row(ple_gate_b[i]),
                 ple_w[i].astype(BF16)).reshape(b, s, d)
    return h
```

```python
import functools
import math

import numpy as np
import jax
import jax.numpy as jnp
from jax import lax
from jax.experimental import pallas as pl
from jax.experimental.pallas import tpu as pltpu

D_MODEL = 1024
DEPTH = 4
CHUNK = 64
PLE_DIM = 256
N_BUCKETS = 32
MAX_DISTANCE = 256
ROPE_THETA = 10000.0
LN_EPS = 1e-5
RMS_EPS = 1e-6
H_A, A_NOPE, A_ROPE, A_V, A_Q_RANK, A_KV_RANK = 8, 64, 32, 64, 512, 256
A_SCALE = (A_NOPE + A_ROPE) ** -0.5
H_B, B_DH = 8, 32
B_SCALE = B_DH ** -0.5
H_C, KV_C, C_DH = 16, 4, 64
C_GROUP = H_C // KV_C
C_SCALE = C_DH ** -0.5
H_IDX, IDX_DH, IDX_ROPE = 16, 64, 32
IDX_SCALE = IDX_DH ** -0.5
TOPK_MAX = 256
P_HEADS, N_KEYS, P_DQ, P_TOPK = 8, 128, 256, 16
P_DHALF = P_DQ // 2
N_EXPERTS = N_KEYS * N_KEYS
DN_ALPHA = (2 * DEPTH) ** 0.25

LANES = 128
ATT_BLK = 256
ATT_ROWS = 256
ATT_AHEAD = 2
DSA_PAIRS = 4
PEER_TE = 8 * N_KEYS
VMEM_LIMIT = 52 * 1024 * 1024

LOG2E = math.log2(math.e)

BF16 = jnp.bfloat16
F32 = jnp.float32
NEG_INF = float("-inf")
M_INIT = -1e30
INT_MIN = np.int32(-2 ** 31)


def _cparams(*sem):
    return pltpu.CompilerParams(dimension_semantics=sem, vmem_limit_bytes=VMEM_LIMIT)


def _full(shape):
    n = len(shape)
    return pl.BlockSpec(shape, lambda *_: (0,) * n)


def _smem():
    return pl.BlockSpec(memory_space=pltpu.SMEM)


def _rms(v, g):
    return v * lax.rsqrt(jnp.mean(v * v, axis=-1, keepdims=True) + RMS_EPS) * g


def _layer_norm(y, g, b):
    mu = jnp.mean(y, axis=-1, keepdims=True)
    yc = y - mu
    var = jnp.mean(yc * yc, axis=-1, keepdims=True)
    return yc * lax.rsqrt(var + LN_EPS) * g + b


def _dot(a, b):
    return jnp.dot(a, b, preferred_element_type=F32)


def _dot_nt(a, b):
    return lax.dot_general(a, b, (((1,), (1,)), ((), ())), preferred_element_type=F32)


def _even_proj_body(h_ref, cos_ref, sin_ref, w_in_ref, w_uq_ref, w_ukv_ref, qg_ref, kvg_ref,
                    qn_ref, qr_ref, kn_ref, kr_ref, va_ref, qd_ref, kd_ref, vd_ref):
    x = h_ref[0].astype(BF16)
    acc = _dot(x, w_in_ref[...])
    cos = cos_ref[0]
    sin = sin_ref[0]
    cq = _rms(acc[:, 0:512], qg_ref[...]).astype(BF16)
    qa = _dot(cq, w_uq_ref[...])
    qn_ref[0] = (qa[:, 0:512] * (A_SCALE * LOG2E)).astype(BF16)
    cos2 = jnp.concatenate([cos, cos], axis=1)
    sin2 = jnp.concatenate([sin, sin], axis=1)
    qr_ref[0] = ((qa[:, 512:768] * cos2 + qa[:, 768:1024] * sin2) * (A_SCALE * LOG2E)).astype(BF16)
    ckv = _rms(acc[:, 512:768], kvg_ref[...]).astype(BF16)
    kv = _dot(ckv, w_ukv_ref[...])
    ones_hi = (lax.broadcasted_iota(jnp.int32, (1, 1024), 1) % LANES >= LANES // 2).astype(F32)
    kn_ref[0] = kv[:, 0:512].astype(BF16)
    va_ref[0] = (kv[:, 512:1536] + ones_hi).astype(BF16)
    qd_ref[0] = (acc[:, 768:1280] * (B_SCALE * LOG2E)).astype(BF16)
    kd_ref[0] = acc[:, 1280:1792].astype(BF16)
    vd_ref[0] = (acc[:, 1792:2816] + ones_hi).astype(BF16)
    kr_ref[0] = (acc[:, 2816:2944] * cos + acc[:, 2944:3072] * sin).astype(BF16)


def _even_proj(h, cos, sin, w_in, w_uq, w_ukv, qg, kvg):
    b, s, d = h.shape
    tm = min(512, s)
    tok = lambda w: pl.BlockSpec((1, tm, w), lambda i, j: (i, j, 0))
    widths = (512, 256, 512, 128, 1024, 512, 512, 1024)
    return pl.pallas_call(
        _even_proj_body,
        grid=(b, s // tm),
        in_specs=[tok(d), tok(LANES), tok(LANES), _full(w_in.shape), _full(w_uq.shape),
                  _full(w_ukv.shape), _full(qg.shape), _full(kvg.shape)],
        out_specs=[tok(w) for w in widths],
        out_shape=[jax.ShapeDtypeStruct((b, s, w), BF16) for w in widths],
        compiler_params=_cparams("parallel", "parallel"),
        name="even_proj",
    )(h, cos, sin, w_in, w_uq, w_ukv, qg, kvg)


def _flash_update(m_ref, acc_ref, i, rows, s, v):
    m_prev = m_ref[i, rows, :]
    m_new = jnp.maximum(m_prev, jnp.max(s, axis=1, keepdims=True))
    alpha = jnp.exp2(m_prev - m_new)
    p = jnp.exp2(s - jnp.concatenate([m_new] * (s.shape[1] // LANES), axis=1))
    acc_ref[i, rows, :] = alpha * acc_ref[i, rows, :] + _dot(p.astype(BF16), v)
    m_ref[i, rows, :] = m_new


def _row_tiles(blk):
    return [slice(r, r + ATT_ROWS) for r in range(0, blk, ATT_ROWS)]


def _lookahead(tiles, produce, consume):
    pending = [produce(t) for t in tiles[:ATT_AHEAD]]
    for n, t in enumerate(tiles):
        if n + ATT_AHEAD < len(tiles):
            pending.append(produce(tiles[n + ATT_AHEAD]))
        consume(t, pending.pop(0))


def _normalized(acc, hh):
    r = pltpu.roll(acc, LANES // 2, 1)
    return acc / r if hh == 0 else r / acc


def _chunk_allowed(tq, tk):
    r = lax.broadcasted_iota(jnp.int32, (tq, tk), 0) // CHUNK
    c = lax.broadcasted_iota(jnp.int32, (tq, tk), 1) // CHUNK
    return c <= r


def _even_attn_body(lam_ref, bfar_ref, qn_ref, qr_ref, kn_ref, kr_ref, va_ref, qd_ref, kd_ref, vd_ref,
                    bias_ref, sg_ref, oa_ref, od_ref, m_ref, acc_ref, *, blk, lam_init):
    p = pl.program_id(1)
    qi = pl.program_id(2)
    lane = lax.broadcasted_iota(jnp.int32, (blk, LANES), 1)
    qn = qn_ref[0]
    qr = qr_ref[0]
    qd = qd_ref[0]
    zero = jnp.zeros_like(qn)
    streams = []
    for hh in range(2):
        lo = 64 * hh
        r0 = 32 * (2 * (p % 2) + hh)
        qn_m = jnp.where((lane >= lo) & (lane < lo + 64), qn, zero)
        qr_m = jnp.where((lane >= r0) & (lane < r0 + 32), qr, zero)
        streams.append((jnp.concatenate([qn_m, qr_m], axis=1), "a", hh))
        streams.append((jnp.where((lane >= lo) & (lane < lo + 32), qd, zero), "d", hh))
        streams.append((jnp.where((lane >= lo + 32) & (lane < lo + 64), qd, zero), "d", hh))
    allowed = _chunk_allowed(blk, blk)

    m_ref[...] = jnp.full(m_ref.shape, M_INIT, F32)
    acc_ref[...] = jnp.zeros(acc_ref.shape, F32)

    def block(j, near):
        ks = pl.multiple_of(j * blk, blk)
        ka = jnp.concatenate([kn_ref[0, pl.ds(ks, blk), :], kr_ref[0, pl.ds(ks, blk), :]], axis=1)
        kd = kd_ref[0, pl.ds(ks, blk), :]
        def scores(tile):
            i, rows = tile
            q, kind, hh = streams[i]
            if kind == "a":
                s = _dot_nt(q[rows], ka)
            else:
                s = _dot_nt(q[rows], kd)
                s = s + (bfar_ref[2 * p + hh] if near is None else bias_ref[0, hh, near, rows, :])
            if near == 0:
                s = jnp.where(allowed[rows], s, NEG_INF)
            return s

        def update(tile, s):
            i, rows = tile
            _, kind, hh = streams[i]
            v_ref = va_ref if kind == "a" else vd_ref
            _flash_update(m_ref, acc_ref, i, rows, s, v_ref[0, pl.ds(ks, blk), hh * LANES:(hh + 1) * LANES])

        _lookahead([(i, rows) for i in range(len(streams)) for rows in _row_tiles(blk)], scores, update)

    def far(j, carry):
        block(j, None)
        return carry

    lax.fori_loop(0, jnp.maximum(qi - 1, 0), far, 0)

    @pl.when(qi >= 1)
    def _previous():
        block(qi - 1, 1)

    block(qi, 0)

    lam = lam_ref[0]
    first = lane < 64
    o = [_normalized(acc_ref[i], hh) for i, (_, _, hh) in enumerate(streams)]
    oa_ref[0] = jnp.where(first, o[0], o[3]).astype(BF16)
    od = jnp.where(first, o[1] - lam * o[2], o[4] - lam * o[5])
    sq = od * od
    ms0 = jnp.sum(jnp.where(first, sq, 0.0), axis=1, keepdims=True)
    ms1 = jnp.sum(jnp.where(first, 0.0, sq), axis=1, keepdims=True)
    ms = jnp.where(first, ms0, ms1) * (1.0 / (2 * B_DH))
    od_ref[0] = (od * lax.rsqrt(ms + RMS_EPS) * sg_ref[...] * (1.0 - lam_init)).astype(BF16)


def _even_attn(lam, bfar, qn, qr, kn, kr, va, qd, kd, vd, bias, sg, lam_init):
    b, s, _ = qn.shape
    blk = ATT_BLK
    qblk = lambda f: pl.BlockSpec((1, blk, LANES), f)
    kv = lambda f: pl.BlockSpec((1, s, LANES), f)
    vv = lambda f: pl.BlockSpec((1, s, 2 * LANES), f)
    body = functools.partial(_even_attn_body, blk=blk, lam_init=lam_init)
    return pl.pallas_call(
        body,
        grid=(b, H_A // 2, s // blk),
        in_specs=[_smem(), _smem(),
                  qblk(lambda i, p, q: (i, q, p)), qblk(lambda i, p, q: (i, q, p // 2)),
                  kv(lambda i, p, q: (i, 0, p)), kv(lambda i, p, q: (i, 0, 0)), vv(lambda i, p, q: (i, 0, p)),
                  qblk(lambda i, p, q: (i, q, p)), kv(lambda i, p, q: (i, 0, p)), vv(lambda i, p, q: (i, 0, p)),
                  pl.BlockSpec((1, 2, 2, blk, blk), lambda i, p, q: (p, 0, 0, 0, 0)),
                  _full(sg.shape)],
        out_specs=[qblk(lambda i, p, q: (i, q, p)), qblk(lambda i, p, q: (i, q, p))],
        out_shape=[jax.ShapeDtypeStruct((b, s, 512), BF16)] * 2,
        scratch_shapes=[pltpu.VMEM((6, blk, LANES), F32), pltpu.VMEM((6, blk, LANES), F32)],
        compiler_params=_cparams("parallel", "parallel", "arbitrary"),
        name="even_attn",
    )(lam, bfar, qn, qr, kn, kr, va, qd, kd, vd, bias, sg)


def _oproj_ln_body(*refs, n_in):
    o_refs = refs[:n_in]
    w_refs = refs[n_in:2 * n_in]
    h_ref, g_ref, b_ref, out_ref = refs[2 * n_in:]
    mix = _dot(o_refs[0][...], w_refs[0][...])
    for o_ref, w_ref in zip(o_refs[1:], w_refs[1:]):
        mix = mix + _dot(o_ref[...], w_ref[...])
    out_ref[...] = _layer_norm(DN_ALPHA * h_ref[...] + mix, g_ref[...], b_ref[...])


def _oproj_ln(os_, ws, h, g, b):
    t, d = h.shape
    tm = min(512, t)
    tok = lambda w: pl.BlockSpec((tm, w), lambda i: (i, 0))
    return pl.pallas_call(
        functools.partial(_oproj_ln_body, n_in=len(os_)),
        grid=(t // tm,),
        in_specs=[tok(o.shape[1]) for o in os_] + [_full(w.shape) for w in ws]
                 + [tok(d), _full(g.shape), _full(b.shape)],
        out_specs=tok(d),
        out_shape=jax.ShapeDtypeStruct((t, d), F32),
        compiler_params=_cparams("parallel"),
        name="oproj_ln",
    )(*os_, *ws, h, g, b)


def _odd_proj_body(h_ref, cos_ref, sin_ref, w_ref, q_ref, k_ref, v_ref, qi_ref, ki_ref, wi_ref):
    x = h_ref[0].astype(BF16)
    acc = _dot(x, w_ref[...])
    cos = cos_ref[0]
    sin = sin_ref[0]
    q_ref[0] = (acc[:, 0:1024] * (C_SCALE * LOG2E)).astype(BF16)
    k_ref[0] = acc[:, 1024:1280].astype(BF16)
    ones_hi = (lax.broadcasted_iota(jnp.int32, (1, 512), 1) % LANES >= LANES // 2).astype(F32)
    v_ref[0] = (acc[:, 1280:1792] + ones_hi).astype(BF16)
    cos8 = jnp.concatenate([cos] * 8, axis=1)
    sin8 = jnp.concatenate([sin] * 8, axis=1)
    qi_ref[0] = ((acc[:, 1792:2816] * cos8 + acc[:, 2816:3840] * sin8) * IDX_SCALE).astype(BF16)
    ki_ref[0] = (acc[:, 3840:3968] * cos + acc[:, 3968:4096] * sin).astype(BF16)
    wi_ref[0] = acc[:, 4096:4224] * (H_IDX ** -0.5)


def _odd_proj(h, cos, sin, w):
    b, s, d = h.shape
    tm = min(256, s)
    tok = lambda wd: pl.BlockSpec((1, tm, wd), lambda i, j: (i, j, 0))
    widths = (1024, 256, 512, 1024, 128, 128)
    dts = (BF16, BF16, BF16, BF16, BF16, F32)
    return pl.pallas_call(
        _odd_proj_body,
        grid=(b, s // tm),
        in_specs=[tok(d), tok(LANES), tok(LANES), _full(w.shape)],
        out_specs=[tok(wd) for wd in widths],
        out_shape=[jax.ShapeDtypeStruct((b, s, wd), dt) for wd, dt in zip(widths, dts)],
        compiler_params=_cparams("parallel", "parallel"),
        name="odd_proj",
    )(h, cos, sin, w)


def _dsa_body(bfar_ref, qi_ref, ki_ref, wi_ref, q_ref, k_ref, v0_ref, v1_ref, bias_ref, o_ref,
              key_ref, madd_ref, m_ref, acc_ref, *, blk, n_sel):
    qb = pl.program_id(1)
    hb = pl.program_id(2)
    lane = lax.broadcasted_iota(jnp.int32, (blk, LANES), 1)
    first = lane < 64

    @pl.when(hb == 0)
    def _select():
        wt = jnp.transpose(wi_ref[0])
        kc = lax.broadcasted_iota(jnp.int32, (blk, blk), 0) // CHUNK
        qc = lax.broadcasted_iota(jnp.int32, (blk, blk), 1) // CHUNK
        chunk_mask = jnp.where(kc <= qc, 0.0, NEG_INF)
        qis = []
        for hp in range(H_IDX // 2):
            qp = qi_ref[0, :, hp * LANES:(hp + 1) * LANES]
            qis.append(jnp.where(first, qp, jnp.zeros_like(qp)))
            qis.append(jnp.where(first, jnp.zeros_like(qp), qp))

        def score(j, _):
            ks = pl.multiple_of(j * blk, blk)
            for rows in (slice(0, blk // 2), slice(blk // 2, blk)):
                kib = ki_ref[0, pl.ds(ks + rows.start, blk // 2), :]
                sc = jnp.zeros((blk // 2, blk), F32)
                for h in range(H_IDX):
                    sc = sc + jnp.maximum(_dot_nt(kib, qis[h]), 0.0) * wt[h:h + 1, :]
                sc = sc + jnp.where(j < qb, 0.0, chunk_mask[rows])
                bits = pltpu.bitcast(sc, jnp.int32)
                key_ref[j, rows, :] = jnp.where(bits < 0, bits ^ np.int32(0x7FFFFFFF), bits)
            return 0

        lax.fori_loop(0, qb + 1, score, 0)

        def count_ge(cand):
            def body(j, part):
                sel = jnp.where(key_ref[j] >= cand, 1.0, 0.0)
                return part + jnp.sum(sel.reshape(blk // 8, 8, blk), axis=0)
            part = lax.fori_loop(0, qb + 1, body, jnp.zeros((8, blk), F32))
            return jnp.sum(part, axis=0, keepdims=True)

        n = float(n_sel)
        zero_i = jnp.zeros((1, blk), jnp.int32)
        ans = jnp.where(count_ge(zero_i) >= n, zero_i, zero_i + INT_MIN)

        def bit_step(t, ans):
            cand = ans | lax.shift_left(jnp.int32(1), 30 - t)
            return jnp.where(count_ge(cand) >= n, cand, ans)

        thr = lax.fori_loop(0, 31, bit_step, ans)

        def make_mask(j, _):
            sel = jnp.where(key_ref[j] >= thr, 0.0, NEG_INF) + jnp.where(j < qb, 0.0, chunk_mask)
            madd_ref[j] = jnp.transpose(sel)
            return 0

        lax.fori_loop(0, qb + 1, make_mask, 0)

    heads = []
    for g in range(DSA_PAIRS):
        q = q_ref[0, :, g * LANES:(g + 1) * LANES]
        zq = jnp.zeros_like(q)
        heads.append((jnp.where(first, q, zq), g, 0))
        heads.append((jnp.where(first, zq, q), g, 1))
    v_refs = (v0_ref, v1_ref)
    m_ref[...] = jnp.full(m_ref.shape, M_INIT, F32)
    acc_ref[...] = jnp.zeros(acc_ref.shape, F32)

    def block(j, near):
        ks = pl.multiple_of(j * blk, blk)
        kb = k_ref[0, pl.ds(ks, blk), :]

        def scores(tile):
            i, rows = tile
            q, g, hh = heads[i]
            s = _dot_nt(q[rows], kb) + madd_ref[j, rows, :]
            if near is None:
                return s + bfar_ref[2 * DSA_PAIRS * hb + 2 * g + hh]
            return s + bias_ref[0, g, hh, near, rows, :]

        def update(tile, s):
            i, rows = tile
            _flash_update(m_ref, acc_ref, i, rows, s, v_refs[heads[i][2]][0, pl.ds(ks, blk), :])

        _lookahead([(i, rows) for i in range(len(heads)) for rows in _row_tiles(blk)], scores, update)

    def far(j, carry):
        block(j, None)
        return carry

    lax.fori_loop(0, jnp.maximum(qb - 1, 0), far, 0)

    @pl.when(qb >= 1)
    def _previous():
        block(qb - 1, 1)

    block(qb, 0)
    for g in range(DSA_PAIRS):
        o = jnp.where(first, _normalized(acc_ref[2 * g], 0), _normalized(acc_ref[2 * g + 1], 1))
        o_ref[0, :, g * LANES:(g + 1) * LANES] = o.astype(BF16)


def _dsa_attn(bfar, qi, ki, wi, q, k, v, bias, n_sel):
    b, s, _ = q.shape
    blk = ATT_BLK
    nk = s // blk
    nhb = H_C // (2 * DSA_PAIRS)
    width = DSA_PAIRS * LANES
    kvp = lambda hb: hb * DSA_PAIRS // C_GROUP
    bias = bias.reshape(nhb, DSA_PAIRS, 2, 2, blk, blk)
    body = functools.partial(_dsa_body, blk=blk, n_sel=n_sel)
    return pl.pallas_call(
        body,
        grid=(b, nk, nhb),
        in_specs=[_smem(),
                  pl.BlockSpec((1, blk, 1024), lambda i, qb, hb: (i, qb, 0)),
                  pl.BlockSpec((1, s, LANES), lambda i, qb, hb: (i, 0, 0)),
                  pl.BlockSpec((1, blk, LANES), lambda i, qb, hb: (i, qb, 0)),
                  pl.BlockSpec((1, blk, width), lambda i, qb, hb: (i, qb, hb)),
                  pl.BlockSpec((1, s, LANES), lambda i, qb, hb: (i, 0, kvp(hb))),
                  pl.BlockSpec((1, s, LANES), lambda i, qb, hb: (i, 0, 2 * kvp(hb))),
                  pl.BlockSpec((1, s, LANES), lambda i, qb, hb: (i, 0, 2 * kvp(hb) + 1)),
                  pl.BlockSpec((1, DSA_PAIRS, 2, 2, blk, blk), lambda i, qb, hb: (hb, 0, 0, 0, 0, 0))],
        out_specs=pl.BlockSpec((1, blk, width), lambda i, qb, hb: (i, qb, hb)),
        out_shape=jax.ShapeDtypeStruct((b, s, H_C * C_DH), BF16),
        scratch_shapes=[pltpu.VMEM((nk, blk, blk), jnp.int32),
                        pltpu.VMEM((nk, blk, blk), F32),
                        pltpu.VMEM((2 * DSA_PAIRS, blk, LANES), F32),
                        pltpu.VMEM((2 * DSA_PAIRS, blk, LANES), F32)],
        compiler_params=_cparams("parallel", "arbitrary", "arbitrary"),
        name="dsa_attn",
    )(bfar, qi, ki, wi, q, k, v, v, bias)


def _top_rows(cur, n):
    vals = []
    for _ in range(n):
        m = jnp.max(cur, axis=0, keepdims=True)
        vals.append(m)
        cur = jnp.where(cur == m, NEG_INF, cur)
    return vals


def _peer_route_body(h_ref, wq_ref, k1_ref, k2_ref, s2_ref, e2_ref, th_ref, e1_ref, q_scr):
    tm = h_ref.shape[0]
    q = _dot(h_ref[...].astype(BF16), wq_ref[...]).astype(BF16)
    for h in range(P_HEADS):
        q_scr[h] = q[:, h * P_DQ:(h + 1) * P_DQ]
    row16 = lax.broadcasted_iota(jnp.int32, (P_TOPK, tm), 0)

    def head(h):
        qh = q_scr[h]
        s1 = _dot_nt(k1_ref[...], qh[:, 0:P_DHALF])
        s2 = _dot_nt(k2_ref[...], qh[:, P_DHALF:P_DQ])
        t1 = _top_rows(s1, P_TOPK + 1)
        t2 = _top_rows(s2, P_TOPK + 1)
        t1m = jnp.zeros((P_TOPK, tm), F32)
        t2m = jnp.zeros((P_TOPK, tm), F32)
        for r in range(P_TOPK):
            t1m = jnp.where(row16 == r, t1[r], t1m)
            t2m = jnp.where(row16 == r, t2[r], t2m)
        cand = jnp.concatenate([t1[r] + t2m for r in range(4)]
                               + [t1[r] + t2m[0:8] for r in range(4, 8)]
                               + [t1m[8:16] + t2[0]], axis=0)
        c = _top_rows(cand, P_TOPK + 1)
        c17 = jnp.maximum(c[P_TOPK], jnp.maximum(t1[P_TOPK] + t2[0], t1[0] + t2[P_TOPK]))
        tau = 0.5 * (c[P_TOPK - 1] + c17)
        z = jnp.sum(jnp.where(cand >= tau, jnp.exp(cand - (t1[0] + t2[0])), 0.0), axis=0, keepdims=True)
        outs = ((s2_ref, s2), (e2_ref, jnp.exp(s2 - t2[0])), (th_ref, tau - s1),
                (e1_ref, jnp.exp(s1 - t1[0]) / z))
        for ref, val in outs:
            ref[h] = val

    def head_pair(hp, carry):
        head(2 * hp)
        head(2 * hp + 1)
        return carry

    lax.fori_loop(0, P_HEADS // 2, head_pair, 0)


def _peer_route(h1, wq, k1, k2):
    t, d = h1.shape
    tm = min(256, t)
    out = pl.BlockSpec((P_HEADS, N_KEYS, tm), lambda i: (0, 0, i))
    return pl.pallas_call(
        _peer_route_body,
        grid=(t // tm,),
        in_specs=[pl.BlockSpec((tm, d), lambda i: (i, 0)), _full(wq.shape), _full(k1.shape), _full(k2.shape)],
        out_specs=[out] * 4,
        out_shape=[jax.ShapeDtypeStruct((P_HEADS, N_KEYS, t), F32)] * 4,
        scratch_shapes=[pltpu.VMEM((P_HEADS, tm, P_DQ), BF16)],
        compiler_params=_cparams("parallel"),
        name="peer_route",
    )(h1, wq, k1, k2)


def _peer_main_body(h1_ref, u_ref, vt_ref, s2_ref, e2_ref, th_ref, e1_ref, g_ref, b_ref, out_ref,
                    xt_ref, acc_ref, at_ref, gw_ref):
    e = pl.program_id(1)
    tm = h1_ref.shape[0]
    nsl = PEER_TE // N_KEYS

    @pl.when(e == 0)
    def _init():
        xt_ref[...] = jnp.transpose(h1_ref[...]).astype(BF16)
        acc_ref[...] = jnp.zeros_like(acc_ref)

    i0 = pl.multiple_of(e * nsl, nsl)
    quarter = 2 * N_KEYS

    def up_proj(q):
        rows = slice(q * quarter, (q + 1) * quarter)
        at_ref[rows, :] = _dot(u_ref[rows, :], xt_ref[...])

    def gate(q):
        slabs = (2 * q, 2 * q + 1)
        for c in range(tm // LANES):
            ls = slice(c * LANES, (c + 1) * LANES)
            ws = [jnp.zeros((N_KEYS, LANES), F32) for _ in slabs]
            for h in range(P_HEADS):
                s2t = s2_ref[h, :, ls]
                e2t = e2_ref[h, :, ls]
                thg = th_ref[h, pl.ds(i0, nsl), ls]
                e1g = e1_ref[h, pl.ds(i0, nsl), ls]
                for k, ii in enumerate(slabs):
                    ws[k] = ws[k] + jnp.where(s2t >= thg[ii:ii + 1, :], e2t, 0.0) * e1g[ii:ii + 1, :]
            for k, ii in enumerate(slabs):
                rows = slice(ii * N_KEYS, (ii + 1) * N_KEYS)
                gw_ref[rows, ls] = jax.nn.gelu(at_ref[rows, ls].astype(BF16)) * ws[k].astype(BF16)

    half = PEER_TE // 2
    partial = []
    up_proj(0)
    for q in range(4):
        if q + 1 < 4:
            up_proj(q + 1)
        gate(q)
        if q % 2 == 1:
            ks = slice((q // 2) * half, (q // 2 + 1) * half)
            partial.append(_dot(vt_ref[0, :, ks], gw_ref[ks, :]))
    acc_ref[...] += partial[0] + partial[1]

    @pl.when(e == pl.num_programs(1) - 1)
    def _finish():
        y = DN_ALPHA * h1_ref[...] + jnp.transpose(acc_ref[...])
        out_ref[...] = _layer_norm(y, g_ref[...], b_ref[...])


def _peer_main(h1, u, vt, s2, e2, th, e1, g, b):
    t, d = h1.shape
    tm = min(512, t)
    route = pl.BlockSpec((P_HEADS, N_KEYS, tm), lambda i, e: (0, 0, i))
    return pl.pallas_call(
        _peer_main_body,
        grid=(t // tm, N_EXPERTS // PEER_TE),
        in_specs=[pl.BlockSpec((tm, d), lambda i, e: (i, 0)),
                  pl.BlockSpec((PEER_TE, d), lambda i, e: (e, 0)),
                  pl.BlockSpec((1, d, PEER_TE), lambda i, e: (e, 0, 0)),
                  route, route, route, route, _full(g.shape), _full(b.shape)],
        out_specs=pl.BlockSpec((tm, d), lambda i, e: (i, 0)),
        out_shape=jax.ShapeDtypeStruct((t, d), F32),
        scratch_shapes=[pltpu.VMEM((d, tm), BF16), pltpu.VMEM((d, tm), F32),
                        pltpu.VMEM((PEER_TE, tm), F32), pltpu.VMEM((PEER_TE, tm), BF16)],
        compiler_params=_cparams("parallel", "arbitrary"),
        name="peer_main",
    )(h1, u, vt, s2, e2, th, e1, g, b)


def _ple_body(h_ref, p_ref, gw_ref, gb_ref, pw_ref, out_ref):
    h = h_ref[...]
    gate = jax.nn.sigmoid(_dot(h.astype(BF16), gw_ref[...]) + gb_ref[...])
    out_ref[...] = h + gate * _dot(p_ref[...].astype(BF16), pw_ref[...])


def _ple(h, p, gw, gb, pw):
    t, d = h.shape
    tm = min(512, t)
    return pl.pallas_call(
        _ple_body,
        grid=(t // tm,),
        in_specs=[pl.BlockSpec((tm, d), lambda i: (i, 0)), pl.BlockSpec((tm, PLE_DIM), lambda i: (i, 0)),
                  _full(gw.shape), _full(gb.shape), _full(pw.shape)],
        out_specs=pl.BlockSpec((tm, d), lambda i: (i, 0)),
        out_shape=jax.ShapeDtypeStruct((t, d), F32),
        compiler_params=_cparams("parallel"),
        name="ple",
    )(h, p, gw, gb, pw)


def _t5_bucket(rel):
    nb = N_BUCKETS // 2
    max_exact = nb // 2
    n = jnp.abs(rel)
    large = max_exact + (jnp.log(jnp.maximum(n, 1).astype(F32) / max_exact)
                         / math.log(MAX_DISTANCE / max_exact) * (nb - max_exact)).astype(jnp.int32)
    large = jnp.minimum(large, nb - 1)
    return jnp.where(rel > 0, nb, 0) + jnp.where(n < max_exact, n, large)


def _bias_tables(tab, heads, blk):
    r = jnp.arange(blk, dtype=jnp.int32)[:, None]
    c = jnp.arange(blk, dtype=jnp.int32)[None, :]
    rel = jnp.stack([c - r, c - r - blk])
    onehot = (_t5_bucket(rel).reshape(1, -1) == jnp.arange(N_BUCKETS, dtype=jnp.int32)[:, None]).astype(F32)
    tiles = jnp.dot(tab[:, jnp.asarray(heads)].T.astype(F32), onehot, precision=lax.Precision.HIGHEST)
    tiles = tiles.reshape(len(heads) // 2, 2, 2, blk, blk)
    far = tab[_t5_bucket(jnp.int32(-(blk + 1)))][jnp.asarray(heads)]
    return tiles.astype(F32) * LOG2E, far.astype(F32) * LOG2E


def _pad_heads(w, n_heads):
    rows = w.shape[0]
    w = w.reshape(rows, n_heads, LANES // 2)
    return jnp.concatenate([w, jnp.zeros_like(w)], axis=-1).reshape(rows, n_heads * LANES)


def _swap_half(w):
    half = w.shape[-1] // 2
    return jnp.concatenate([-w[..., half:], w[..., :half]], axis=-1)


def _even_weights(w_in, w_uq, w_ukv):
    c_q, c_kv, k_r = w_in[:, 0:512], w_in[:, 512:768], w_in[:, 768:800]
    qk_d, v_d = w_in[:, 800:1824], w_in[:, 1824:2336]
    w_in2 = jnp.concatenate([c_q, c_kv, qk_d, _pad_heads(v_d, H_B), jnp.tile(k_r, (1, 4)),
                             jnp.tile(_swap_half(k_r), (1, 4))], axis=1)
    uq = w_uq.reshape(A_Q_RANK, H_A, A_NOPE + A_ROPE)
    rope = uq[:, :, A_NOPE:]
    w_uq2 = jnp.concatenate([uq[:, :, :A_NOPE].reshape(A_Q_RANK, -1), rope.reshape(A_Q_RANK, -1),
                             _swap_half(rope).reshape(A_Q_RANK, -1)], axis=1)
    ukv = w_ukv.reshape(A_KV_RANK, H_A, A_NOPE + A_V)
    w_ukv2 = jnp.concatenate([ukv[:, :, :A_NOPE].reshape(A_KV_RANK, -1),
                              _pad_heads(ukv[:, :, A_NOPE:].reshape(A_KV_RANK, -1), H_A)], axis=1)
    return w_in2.astype(BF16), w_uq2.astype(BF16), w_ukv2.astype(BF16)


_ODD_HEADS = [(2 * pp + a) * C_GROUP + g for pp in range(KV_C // 2) for g in range(C_GROUP) for a in range(2)]


def _odd_weights(w_in, w_o):
    d = w_in.shape[0]
    heads = jnp.asarray(_ODD_HEADS)
    wq = w_in[:, 0:1024].reshape(d, H_C, C_DH)[:, heads].reshape(d, -1)
    wk, wv = w_in[:, 1024:1280], w_in[:, 1280:1536]
    wqi = w_in[:, 1536:2560].reshape(d, H_IDX, IDX_DH)
    wqi_s = jnp.concatenate([_swap_half(wqi[:, :, :IDX_ROPE]), jnp.zeros_like(wqi[:, :, IDX_ROPE:])], axis=-1)
    wki = w_in[:, 2560:2624]
    wki_s = jnp.concatenate([_swap_half(wki[:, :IDX_ROPE]), jnp.zeros_like(wki[:, IDX_ROPE:])], axis=-1)
    wwi = jnp.pad(w_in[:, 2624:2640], ((0, 0), (0, LANES - H_IDX)))
    w2 = jnp.concatenate([wq, wk, _pad_heads(wv, KV_C), wqi.reshape(d, -1), wqi_s.reshape(d, -1),
                          jnp.tile(wki, (1, 2)), jnp.tile(wki_s, (1, 2)), wwi], axis=1)
    w_o2 = w_o.reshape(H_C, C_DH, -1)[heads].reshape(H_C * C_DH, -1)
    return w2.astype(BF16), w_o2.astype(BF16)


def _rope_tables(positions):
    half = A_ROPE // 2
    freqs = ROPE_THETA ** (-jnp.arange(half, dtype=F32) / half)
    ang = positions.astype(F32)[..., None] * freqs
    cos, sin = jnp.cos(ang), jnp.sin(ang)
    cos32 = jnp.concatenate([cos, cos], axis=-1)
    sin32 = jnp.concatenate([sin, sin], axis=-1)
    ones, zeros = jnp.ones_like(cos32), jnp.zeros_like(sin32)
    return (jnp.tile(cos32, (1, 1, 4)), jnp.tile(sin32, (1, 1, 4)),
            jnp.tile(jnp.concatenate([cos32, ones], axis=-1), (1, 1, 2)),
            jnp.tile(jnp.concatenate([sin32, zeros], axis=-1), (1, 1, 2)))


def kernel(x, p, positions, rel_bias, ev_w_in, ev_w_uq, ev_w_ukv, ev_q_norm, ev_kv_norm, ev_lam_q1, ev_lam_k1, ev_lam_q2, ev_lam_k2, ev_subln, ev_w_o, od_w_in, od_w_o, ln1_g, ln1_b, ln2_g, ln2_b, peer_w_q, peer_k1, peer_k2, peer_u, peer_v, ple_w, ple_gate_w, ple_gate_b):
    b, s, d = x.shape
    t = b * s
    assert s % ATT_BLK == 0 and d == D_MODEL
    n_sel = min(TOPK_MAX, s // 4)
    cos_f, sin_f, cos_p, sin_p = _rope_tables(positions)
    bias_b, bfar_b = _bias_tables(rel_bias[:, :H_B], list(range(H_B)), ATT_BLK)
    bias_c, bfar_c = _bias_tables(rel_bias[:, H_B:], _ODD_HEADS, ATT_BLK)
    row = lambda a: a.reshape(1, -1).astype(F32)

    h = x
    for i in range(DEPTH):
        j = i // 2
        if i % 2 == 0:
            lam_init = 0.8 - 0.6 * math.exp(-0.3 * i)
            lam = (jnp.exp(jnp.sum(ev_lam_q1[j] * ev_lam_k1[j], dtype=F32))
                   - jnp.exp(jnp.sum(ev_lam_q2[j] * ev_lam_k2[j], dtype=F32)) + lam_init).reshape(1)
            w_in2, w_uq2, w_ukv2 = _even_weights(ev_w_in[j], ev_w_uq[j], ev_w_ukv[j])
            qn, qr, kn, kr, va, qd, kd, vd = _even_proj(h, cos_f, sin_f, w_in2, w_uq2, w_ukv2,
                                                        row(ev_q_norm[j]), row(ev_kv_norm[j]))
            sg = row(jnp.tile(ev_subln[j], 2))
            oa, od = _even_attn(lam, bfar_b, qn, qr, kn, kr, va, qd, kd, vd, bias_b, sg, lam_init)
            w_o = ev_w_o[j].astype(BF16)
            os_ = [oa.reshape(t, -1), od.reshape(t, -1)]
            ws = [w_o[:H_A * A_V], w_o[H_A * A_V:]]
        else:
            w_in2, w_o2 = _odd_weights(od_w_in[j], od_w_o[j])
            q, k, v, qi, ki, wi = _odd_proj(h, cos_p, sin_p, w_in2)
            o = _dsa_attn(bfar_c, qi, ki, wi, q, k, v, bias_c, n_sel)
            os_, ws = [o.reshape(t, -1)], [w_o2]
        h1 = _oproj_ln(os_, ws, h.reshape(t, d), row(ln1_g[i]), row(ln1_b[i]))
        s2, e2, th, e1 = _peer_route(h1, peer_w_q[i].astype(BF16), peer_k1[i].astype(BF16),
                                     peer_k2[i].astype(BF16))
        vt = jnp.transpose(peer_v[i].reshape(N_EXPERTS // PEER_TE, PEER_TE, d), (0, 2, 1)).astype(BF16)
        h2 = _peer_main(h1, peer_u[i].astype(BF16), vt,
                        s2, e2, th, e1, row(ln2_g[i]), row(ln2_b[i]))
        h = _ple(h2, p[i].reshape(t, PLE_DIM), ple_gate_w[i].astype(BF16), row(ple_gate_b[i]),
                 ple_w[i].astype(BF16)).reshape(b, s, d)
    return h
```

```python
import functools
import math

import numpy as np
import jax
import jax.numpy as jnp
from jax import lax
from jax.experimental import pallas as pl
from jax.experimental.pallas import tpu as pltpu

D_MODEL = 1024
DEPTH = 4
CHUNK = 64
PLE_DIM = 256
N_BUCKETS = 32
MAX_DISTANCE = 256
ROPE_THETA = 10000.0
LN_EPS = 1e-5
RMS_EPS = 1e-6
H_A, A_NOPE, A_ROPE, A_V, A_Q_RANK, A_KV_RANK = 8, 64, 32, 64, 512, 256
A_SCALE = (A_NOPE + A_ROPE) ** -0.5
H_B, B_DH = 8, 32
B_SCALE = B_DH ** -0.5
H_C, KV_C, C_DH = 16, 4, 64
C_GROUP = H_C // KV_C
C_SCALE = C_DH ** -0.5
H_IDX, IDX_DH, IDX_ROPE = 16, 64, 32
IDX_SCALE = IDX_DH ** -0.5
TOPK_MAX = 256
P_HEADS, N_KEYS, P_DQ, P_TOPK = 8, 128, 256, 16
P_DHALF = P_DQ // 2
N_EXPERTS = N_KEYS * N_KEYS
DN_ALPHA = (2 * DEPTH) ** 0.25

LANES = 128
ATT_BLK = 256
ATT_ROWS = 256
ATT_AHEAD = 2
DSA_PAIRS = 4
PEER_TE = 8 * N_KEYS
VMEM_LIMIT = 52 * 1024 * 1024

LOG2E = math.log2(math.e)

BF16 = jnp.bfloat16
F32 = jnp.float32
NEG_INF = float("-inf")
M_INIT = -1e30
INT_MIN = np.int32(-2 ** 31)


def _cparams(*sem):
    return pltpu.CompilerParams(dimension_semantics=sem, vmem_limit_bytes=VMEM_LIMIT)


def _full(shape):
    n = len(shape)
    return pl.BlockSpec(shape, lambda *_: (0,) * n)


def _smem():
    return pl.BlockSpec(memory_space=pltpu.SMEM)


def _rms(v, g):
    return v * lax.rsqrt(jnp.mean(v * v, axis=-1, keepdims=True) + RMS_EPS) * g


def _layer_norm(y, g, b):
    mu = jnp.mean(y, axis=-1, keepdims=True)
    yc = y - mu
    var = jnp.mean(yc * yc, axis=-1, keepdims=True)
    return yc * lax.rsqrt(var + LN_EPS) * g + b


def _dot(a, b):
    return jnp.dot(a, b, preferred_element_type=F32)


def _dot_nt(a, b):
    return lax.dot_general(a, b, (((1,), (1,)), ((), ())), preferred_element_type=F32)


def _even_proj_body(h_ref, cos_ref, sin_ref, w_in_ref, w_uq_ref, w_ukv_ref, qg_ref, kvg_ref,
                    qn_ref, qr_ref, kn_ref, kr_ref, va_ref, qd_ref, kd_ref, vd_ref):
    x = h_ref[0].astype(BF16)
    acc = _dot(x, w_in_ref[...])
    cos = cos_ref[0]
    sin = sin_ref[0]
    cq = _rms(acc[:, 0:512], qg_ref[...]).astype(BF16)
    qa = _dot(cq, w_uq_ref[...])
    qn_ref[0] = (qa[:, 0:512] * (A_SCALE * LOG2E)).astype(BF16)
    cos2 = jnp.concatenate([cos, cos], axis=1)
    sin2 = jnp.concatenate([sin, sin], axis=1)
    qr_ref[0] = ((qa[:, 512:768] * cos2 + qa[:, 768:1024] * sin2) * (A_SCALE * LOG2E)).astype(BF16)
    ckv = _rms(acc[:, 512:768], kvg_ref[...]).astype(BF16)
    kv = _dot(ckv, w_ukv_ref[...])
    ones_hi = (lax.broadcasted_iota(jnp.int32, (1, 1024), 1) % LANES >= LANES // 2).astype(F32)
    kn_ref[0] = kv[:, 0:512].astype(BF16)
    va_ref[0] = (kv[:, 512:1536] + ones_hi).astype(BF16)
    qd_ref[0] = (acc[:, 768:1280] * (B_SCALE * LOG2E)).astype(BF16)
    kd_ref[0] = acc[:, 1280:1792].astype(BF16)
    vd_ref[0] = (acc[:, 1792:2816] + ones_hi).astype(BF16)
    kr_ref[0] = (acc[:, 2816:2944] * cos + acc[:, 2944:3072] * sin).astype(BF16)


def _even_proj(h, cos, sin, w_in, w_uq, w_ukv, qg, kvg):
    b, s, d = h.shape
    tm = min(512, s)
    tok = lambda w: pl.BlockSpec((1, tm, w), lambda i, j: (i, j, 0))
    widths = (512, 256, 512, 128, 1024, 512, 512, 1024)
    return pl.pallas_call(
        _even_proj_body,
        grid=(b, s // tm),
        in_specs=[tok(d), tok(LANES), tok(LANES), _full(w_in.shape), _full(w_uq.shape),
                  _full(w_ukv.shape), _full(qg.shape), _full(kvg.shape)],
        out_specs=[tok(w) for w in widths],
        out_shape=[jax.ShapeDtypeStruct((b, s, w), BF16) for w in widths],
        compiler_params=_cparams("parallel", "parallel"),
        name="even_proj",
    )(h, cos, sin, w_in, w_uq, w_ukv, qg, kvg)


def _flash_update(m_ref, acc_ref, i, rows, s, v):
    m_prev = m_ref[i, rows, :]
    m_new = jnp.maximum(m_prev, jnp.max(s, axis=1, keepdims=True))
    alpha = jnp.exp2(m_prev - m_new)
    p = jnp.exp2(s - jnp.concatenate([m_new] * (s.shape[1] // LANES), axis=1))
    acc_ref[i, rows, :] = alpha * acc_ref[i, rows, :] + _dot(p.astype(BF16), v)
    m_ref[i, rows, :] = m_new


def _row_tiles(blk):
    return [slice(r, r + ATT_ROWS) for r in range(0, blk, ATT_ROWS)]


def _lookahead(tiles, produce, consume):
    pending = [produce(t) for t in tiles[:ATT_AHEAD]]
    for n, t in enumerate(tiles):
        if n + ATT_AHEAD < len(tiles):
            pending.append(produce(tiles[n + ATT_AHEAD]))
        consume(t, pending.pop(0))


def _normalized(acc, hh):
    r = pltpu.roll(acc, LANES // 2, 1)
    return acc / r if hh == 0 else r / acc


def _chunk_allowed(tq, tk):
    r = lax.broadcasted_iota(jnp.int32, (tq, tk), 0) // CHUNK
    c = lax.broadcasted_iota(jnp.int32, (tq, tk), 1) // CHUNK
    return c <= r


def _even_attn_body(lam_ref, bfar_ref, qn_ref, qr_ref, kn_ref, kr_ref, va_ref, qd_ref, kd_ref, vd_ref,
                    bias_ref, sg_ref, oa_ref, od_ref, m_ref, acc_ref, *, blk, lam_init):
    p = pl.program_id(1)
    qi = pl.program_id(2)
    lane = lax.broadcasted_iota(jnp.int32, (blk, LANES), 1)
    qn = qn_ref[0]
    qr = qr_ref[0]
    qd = qd_ref[0]
    zero = jnp.zeros_like(qn)
    streams = []
    for hh in range(2):
        lo = 64 * hh
        r0 = 32 * (2 * (p % 2) + hh)
        qn_m = jnp.where((lane >= lo) & (lane < lo + 64), qn, zero)
        qr_m = jnp.where((lane >= r0) & (lane < r0 + 32), qr, zero)
        streams.append((jnp.concatenate([qn_m, qr_m], axis=1), "a", hh))
        streams.append((jnp.where((lane >= lo) & (lane < lo + 32), qd, zero), "d", hh))
        streams.append((jnp.where((lane >= lo + 32) & (lane < lo + 64), qd, zero), "d", hh))
    allowed = _chunk_allowed(blk, blk)

    m_ref[...] = jnp.full(m_ref.shape, M_INIT, F32)
    acc_ref[...] = jnp.zeros(acc_ref.shape, F32)

    def block(j, near):
        ks = pl.multiple_of(j * blk, blk)
        ka = jnp.concatenate([kn_ref[0, pl.ds(ks, blk), :], kr_ref[0, pl.ds(ks, blk), :]], axis=1)
        kd = kd_ref[0, pl.ds(ks, blk), :]
        def scores(tile):
            i, rows = tile
            q, kind, hh = streams[i]
            if kind == "a":
                s = _dot_nt(q[rows], ka)
            else:
                s = _dot_nt(q[rows], kd)
                s = s + (bfar_ref[2 * p + hh] if near is None else bias_ref[0, hh, near, rows, :])
            if near == 0:
                s = jnp.where(allowed[rows], s, NEG_INF)
            return s

        def update(tile, s):
            i, rows = tile
            _, kind, hh = streams[i]
            v_ref = va_ref if kind == "a" else vd_ref
            _flash_update(m_ref, acc_ref, i, rows, s, v_ref[0, pl.ds(ks, blk), hh * LANES:(hh + 1) * LANES])

        _lookahead([(i, rows) for i in range(len(streams)) for rows in _row_tiles(blk)], scores, update)

    def far(j, carry):
        block(j, None)
        return carry

    lax.fori_loop(0, jnp.maximum(qi - 1, 0), far, 0)

    @pl.when(qi >= 1)
    def _previous():
        block(qi - 1, 1)

    block(qi, 0)

    lam = lam_ref[0]
    first = lane < 64
    o = [_normalized(acc_ref[i], hh) for i, (_, _, hh) in enumerate(streams)]
    oa_ref[0] = jnp.where(first, o[0], o[3]).astype(BF16)
    od = jnp.where(first, o[1] - lam * o[2], o[4] - lam * o[5])
    sq = od * od
    ms0 = jnp.sum(jnp.where(first, sq, 0.0), axis=1, keepdims=True)
    ms1 = jnp.sum(jnp.where(first, 0.0, sq), axis=1, keepdims=True)
    ms = jnp.where(first, ms0, ms1) * (1.0 / (2 * B_DH))
    od_ref[0] = (od * lax.rsqrt(ms + RMS_EPS) * sg_ref[...] * (1.0 - lam_init)).astype(BF16)


def _even_attn(lam, bfar, qn, qr, kn, kr, va, qd, kd, vd, bias, sg, lam_init):
    b, s, _ = qn.shape
    blk = ATT_BLK
    qblk = lambda f: pl.BlockSpec((1, blk, LANES), f)
    kv = lambda f: pl.BlockSpec((1, s, LANES), f)
    vv = lambda f: pl.BlockSpec((1, s, 2 * LANES), f)
    body = functools.partial(_even_attn_body, blk=blk, lam_init=lam_init)
    return pl.pallas_call(
        body,
        grid=(b, H_A // 2, s // blk),
        in_specs=[_smem(), _smem(),
                  qblk(lambda i, p, q: (i, q, p)), qblk(lambda i, p, q: (i, q, p // 2)),
                  kv(lambda i, p, q: (i, 0, p)), kv(lambda i, p, q: (i, 0, 0)), vv(lambda i, p, q: (i, 0, p)),
                  qblk(lambda i, p, q: (i, q, p)), kv(lambda i, p, q: (i, 0, p)), vv(lambda i, p, q: (i, 0, p)),
                  pl.BlockSpec((1, 2, 2, blk, blk), lambda i, p, q: (p, 0, 0, 0, 0)),
                  _full(sg.shape)],
        out_specs=[qblk(lambda i, p, q: (i, q, p)), qblk(lambda i, p, q: (i, q, p))],
        out_shape=[jax.ShapeDtypeStruct((b, s, 512), BF16)] * 2,
        scratch_shapes=[pltpu.VMEM((6, blk, LANES), F32), pltpu.VMEM((6, blk, LANES), F32)],
        compiler_params=_cparams("parallel", "parallel", "arbitrary"),
        name="even_attn",
    )(lam, bfar, qn, qr, kn, kr, va, qd, kd, vd, bias, sg)


def _oproj_ln_body(*refs, n_in):
    o_refs = refs[:n_in]
    w_refs = refs[n_in:2 * n_in]
    h_ref, g_ref, b_ref, out_ref = refs[2 * n_in:]
    mix = _dot(o_refs[0][...], w_refs[0][...])
    for o_ref, w_ref in zip(o_refs[1:], w_refs[1:]):
        mix = mix + _dot(o_ref[...], w_ref[...])
    out_ref[...] = _layer_norm(DN_ALPHA * h_ref[...] + mix, g_ref[...], b_ref[...])


def _oproj_ln(os_, ws, h, g, b):
    t, d = h.shape
    tm = min(512, t)
    tok = lambda w: pl.BlockSpec((tm, w), lambda i: (i, 0))
    return pl.pallas_call(
        functools.partial(_oproj_ln_body, n_in=len(os_)),
        grid=(t // tm,),
        in_specs=[tok(o.shape[1]) for o in os_] + [_full(w.shape) for w in ws]
                 + [tok(d), _full(g.shape), _full(b.shape)],
        out_specs=tok(d),
        out_shape=jax.ShapeDtypeStruct((t, d), F32),
        compiler_params=_cparams("parallel"),
        name="oproj_ln",
    )(*os_, *ws, h, g, b)


def _odd_proj_body(h_ref, cos_ref, sin_ref, w_ref, q_ref, k_ref, v_ref, qi_ref, ki_ref, wi_ref):
    x = h_ref[0].astype(BF16)
    acc = _dot(x, w_ref[...])
    cos = cos_ref[0]
    sin = sin_ref[0]
    q_ref[0] = (acc[:, 0:1024] * (C_SCALE * LOG2E)).astype(BF16)
    k_ref[0] = acc[:, 1024:1280].astype(BF16)
    ones_hi = (lax.broadcasted_iota(jnp.int32, (1, 512), 1) % LANES >= LANES // 2).astype(F32)
    v_ref[0] = (acc[:, 1280:1792] + ones_hi).astype(BF16)
    cos8 = jnp.concatenate([cos] * 8, axis=1)
    sin8 = jnp.concatenate([sin] * 8, axis=1)
    qi_ref[0] = ((acc[:, 1792:2816] * cos8 + acc[:, 2816:3840] * sin8) * IDX_SCALE).astype(BF16)
    ki_ref[0] = (acc[:, 3840:3968] * cos + acc[:, 3968:4096] * sin).astype(BF16)
    wi_ref[0] = acc[:, 4096:4224] * (H_IDX ** -0.5)


def _odd_proj(h, cos, sin, w):
    b, s, d = h.shape
    tm = min(256, s)
    tok = lambda wd: pl.BlockSpec((1, tm, wd), lambda i, j: (i, j, 0))
    widths = (1024, 256, 512, 1024, 128, 128)
    dts = (BF16, BF16, BF16, BF16, BF16, F32)
    return pl.pallas_call(
        _odd_proj_body,
        grid=(b, s // tm),
        in_specs=[tok(d), tok(LANES), tok(LANES), _full(w.shape)],
        out_specs=[tok(wd) for wd in widths],
        out_shape=[jax.ShapeDtypeStruct((b, s, wd), dt) for wd, dt in zip(widths, dts)],
        compiler_params=_cparams("parallel", "parallel"),
        name="odd_proj",
    )(h, cos, sin, w)


def _dsa_body(bfar_ref, qi_ref, ki_ref, wi_ref, q_ref, k_ref, v0_ref, v1_ref, bias_ref, o_ref,
              key_ref, madd_ref, m_ref, acc_ref, *, blk, n_sel):
    qb = pl.program_id(1)
    hb = pl.program_id(2)
    lane = lax.broadcasted_iota(jnp.int32, (blk, LANES), 1)
    first = lane < 64

    @pl.when(hb == 0)
    def _select():
        wt = jnp.transpose(wi_ref[0])
        kc = lax.broadcasted_iota(jnp.int32, (blk, blk), 0) // CHUNK
        qc = lax.broadcasted_iota(jnp.int32, (blk, blk), 1) // CHUNK
        chunk_mask = jnp.where(kc <= qc, 0.0, NEG_INF)
        qis = []
        for hp in range(H_IDX // 2):
            qp = qi_ref[0, :, hp * LANES:(hp + 1) * LANES]
            qis.append(jnp.where(first, qp, jnp.zeros_like(qp)))
            qis.append(jnp.where(first, jnp.zeros_like(qp), qp))

        def score(j, _):
            ks = pl.multiple_of(j * blk, blk)
            for rows in (slice(0, blk // 2), slice(blk // 2, blk)):
                kib = ki_ref[0, pl.ds(ks + rows.start, blk // 2), :]
                sc = jnp.zeros((blk // 2, blk), F32)
                for h in range(H_IDX):
                    sc = sc + jnp.maximum(_dot_nt(kib, qis[h]), 0.0) * wt[h:h + 1, :]
                sc = sc + jnp.where(j < qb, 0.0, chunk_mask[rows])
                bits = pltpu.bitcast(sc, jnp.int32)
                key_ref[j, rows, :] = jnp.where(bits < 0, bits ^ np.int32(0x7FFFFFFF), bits)
            return 0

        lax.fori_loop(0, qb + 1, score, 0)

        def count_ge(cand):
            def body(j, part):
                sel = jnp.where(key_ref[j] >= cand, 1.0, 0.0)
                return part + jnp.sum(sel.reshape(blk // 8, 8, blk), axis=0)
            part = lax.fori_loop(0, qb + 1, body, jnp.zeros((8, blk), F32))
            return jnp.sum(part, axis=0, keepdims=True)

        n = float(n_sel)
        zero_i = jnp.zeros((1, blk), jnp.int32)
        ans = jnp.where(count_ge(zero_i) >= n, zero_i, zero_i + INT_MIN)

        def bit_step(t, ans):
            cand = ans | lax.shift_left(jnp.int32(1), 30 - t)
            return jnp.where(count_ge(cand) >= n, cand, ans)

        thr = lax.fori_loop(0, 31, bit_step, ans)

        def make_mask(j, _):
            sel = jnp.where(key_ref[j] >= thr, 0.0, NEG_INF) + jnp.where(j < qb, 0.0, chunk_mask)
            madd_ref[j] = jnp.transpose(sel)
            return 0

        lax.fori_loop(0, qb + 1, make_mask, 0)

    heads = []
    for g in range(DSA_PAIRS):
        q = q_ref[0, :, g * LANES:(g + 1) * LANES]
        zq = jnp.zeros_like(q)
        heads.append((jnp.where(first, q, zq), g, 0))
        heads.append((jnp.where(first, zq, q), g, 1))
    v_refs = (v0_ref, v1_ref)
    m_ref[...] = jnp.full(m_ref.shape, M_INIT, F32)
    acc_ref[...] = jnp.zeros(acc_ref.shape, F32)

    def block(j, near):
        ks = pl.multiple_of(j * blk, blk)
        kb = k_ref[0, pl.ds(ks, blk), :]

        def scores(tile):
            i, rows = tile
            q, g, hh = heads[i]
            s = _dot_nt(q[rows], kb) + madd_ref[j, rows, :]
            if near is None:
                return s + bfar_ref[2 * DSA_PAIRS * hb + 2 * g + hh]
            return s + bias_ref[0, g, hh, near, rows, :]

        def update(tile, s):
            i, rows = tile
            _flash_update(m_ref, acc_ref, i, rows, s, v_refs[heads[i][2]][0, pl.ds(ks, blk), :])

        _lookahead([(i, rows) for i in range(len(heads)) for rows in _row_tiles(blk)], scores, update)

    def far(j, carry):
        block(j, None)
        return carry

    lax.fori_loop(0, jnp.maximum(qb - 1, 0), far, 0)

    @pl.when(qb >= 1)
    def _previous():
        block(qb - 1, 1)

    block(qb, 0)
    for g in range(DSA_PAIRS):
        o = jnp.where(first, _normalized(acc_ref[2 * g], 0), _normalized(acc_ref[2 * g + 1], 1))
        o_ref[0, :, g * LANES:(g + 1) * LANES] = o.astype(BF16)


def _dsa_attn(bfar, qi, ki, wi, q, k, v, bias, n_sel):
    b, s, _ = q.shape
    blk = ATT_BLK
    nk = s // blk
    nhb = H_C // (2 * DSA_PAIRS)
    width = DSA_PAIRS * LANES
    kvp = lambda hb: hb * DSA_PAIRS // C_GROUP
    bias = bias.reshape(nhb, DSA_PAIRS, 2, 2, blk, blk)
    body = functools.partial(_dsa_body, blk=blk, n_sel=n_sel)
    return pl.pallas_call(
        body,
        grid=(b, nk, nhb),
        in_specs=[_smem(),
                  pl.BlockSpec((1, blk, 1024), lambda i, qb, hb: (i, qb, 0)),
                  pl.BlockSpec((1, s, LANES), lambda i, qb, hb: (i, 0, 0)),
                  pl.BlockSpec((1, blk, LANES), lambda i, qb, hb: (i, qb, 0)),
                  pl.BlockSpec((1, blk, width), lambda i, qb, hb: (i, qb, hb)),
                  pl.BlockSpec((1, s, LANES), lambda i, qb, hb: (i, 0, kvp(hb))),
                  pl.BlockSpec((1, s, LANES), lambda i, qb, hb: (i, 0, 2 * kvp(hb))),
                  pl.BlockSpec((1, s, LANES), lambda i, qb, hb: (i, 0, 2 * kvp(hb) + 1)),
                  pl.BlockSpec((1, DSA_PAIRS, 2, 2, blk, blk), lambda i, qb, hb: (hb, 0, 0, 0, 0, 0))],
        out_specs=pl.BlockSpec((1, blk, width), lambda i, qb, hb: (i, qb, hb)),
        out_shape=jax.ShapeDtypeStruct((b, s, H_C * C_DH), BF16),
        scratch_shapes=[pltpu.VMEM((nk, blk, blk), jnp.int32),
                        pltpu.VMEM((nk, blk, blk), F32),
                        pltpu.VMEM((2 * DSA_PAIRS, blk, LANES), F32),
                        pltpu.VMEM((2 * DSA_PAIRS, blk, LANES), F32)],
        compiler_params=_cparams("parallel", "arbitrary", "arbitrary"),
        name="dsa_attn",
    )(bfar, qi, ki, wi, q, k, v, v, bias)


def _top_rows(cur, n):
    vals = []
    for _ in range(n):
        m = jnp.max(cur, axis=0, keepdims=True)
        vals.append(m)
        cur = jnp.where(cur == m, NEG_INF, cur)
    return vals


def _peer_route_body(h_ref, wq_ref, k1_ref, k2_ref, s2_ref, e2_ref, th_ref, e1_ref, q_scr):
    tm = h_ref.shape[0]
    q = _dot(h_ref[...].astype(BF16), wq_ref[...]).astype(BF16)
    for h in range(P_HEADS):
        q_scr[h] = q[:, h * P_DQ:(h + 1) * P_DQ]
    row16 = lax.broadcasted_iota(jnp.int32, (P_TOPK, tm), 0)

    def head(h):
        qh = q_scr[h]
        s1 = _dot_nt(k1_ref[...], qh[:, 0:P_DHALF])
        s2 = _dot_nt(k2_ref[...], qh[:, P_DHALF:P_DQ])
        t1 = _top_rows(s1, P_TOPK + 1)
        t2 = _top_rows(s2, P_TOPK + 1)
        t1m = jnp.zeros((P_TOPK, tm), F32)
        t2m = jnp.zeros((P_TOPK, tm), F32)
        for r in range(P_TOPK):
            t1m = jnp.where(row16 == r, t1[r], t1m)
            t2m = jnp.where(row16 == r, t2[r], t2m)
        cand = jnp.concatenate([t1[r] + t2m for r in range(4)]
                               + [t1[r] + t2m[0:8] for r in range(4, 8)]
                               + [t1m[8:16] + t2[0]], axis=0)
        c = _top_rows(cand, P_TOPK + 1)
        c17 = jnp.maximum(c[P_TOPK], jnp.maximum(t1[P_TOPK] + t2[0], t1[0] + t2[P_TOPK]))
        tau = 0.5 * (c[P_TOPK - 1] + c17)
        z = jnp.sum(jnp.where(cand >= tau, jnp.exp(cand - (t1[0] + t2[0])), 0.0), axis=0, keepdims=True)
        outs = ((s2_ref, s2), (e2_ref, jnp.exp(s2 - t2[0])), (th_ref, tau - s1),
                (e1_ref, jnp.exp(s1 - t1[0]) / z))
        for ref, val in outs:
            ref[h] = val

    def head_pair(hp, carry):
        head(2 * hp)
        head(2 * hp + 1)
        return carry

    lax.fori_loop(0, P_HEADS // 2, head_pair, 0)


def _peer_route(h1, wq, k1, k2):
    t, d = h1.shape
    tm = min(256, t)
    out = pl.BlockSpec((P_HEADS, N_KEYS, tm), lambda i: (0, 0, i))
    return pl.pallas_call(
        _peer_route_body,
        grid=(t // tm,),
        in_specs=[pl.BlockSpec((tm, d), lambda i: (i, 0)), _full(wq.shape), _full(k1.shape), _full(k2.shape)],
        out_specs=[out] * 4,
        out_shape=[jax.ShapeDtypeStruct((P_HEADS, N_KEYS, t), F32)] * 4,
        scratch_shapes=[pltpu.VMEM((P_HEADS, tm, P_DQ), BF16)],
        compiler_params=_cparams("parallel"),
        name="peer_route",
    )(h1, wq, k1, k2)


def _peer_main_body(h1_ref, u_ref, vt_ref, s2_ref, e2_ref, th_ref, e1_ref, g_ref, b_ref, out_ref,
                    xt_ref, acc_ref, at_ref, gw_ref):
    e = pl.program_id(1)
    tm = h1_ref.shape[0]
    nsl = PEER_TE // N_KEYS

    @pl.when(e == 0)
    def _init():
        xt_ref[...] = jnp.transpose(h1_ref[...]).astype(BF16)
        acc_ref[...] = jnp.zeros_like(acc_ref)

    i0 = pl.multiple_of(e * nsl, nsl)
    quarter = 2 * N_KEYS

    def up_proj(q):
        rows = slice(q * quarter, (q + 1) * quarter)
        at_ref[rows, :] = _dot(u_ref[rows, :], xt_ref[...])

    def gate(q):
        slabs = (2 * q, 2 * q + 1)
        for c in range(tm // LANES):
            ls = slice(c * LANES, (c + 1) * LANES)
            ws = [jnp.zeros((N_KEYS, LANES), F32) for _ in slabs]
            for h in range(P_HEADS):
                s2t = s2_ref[h, :, ls]
                e2t = e2_ref[h, :, ls]
                thg = th_ref[h, pl.ds(i0, nsl), ls]
                e1g = e1_ref[h, pl.ds(i0, nsl), ls]
                for k, ii in enumerate(slabs):
                    ws[k] = ws[k] + jnp.where(s2t >= thg[ii:ii + 1, :], e2t, 0.0) * e1g[ii:ii + 1, :]
            for k, ii in enumerate(slabs):
                rows = slice(ii * N_KEYS, (ii + 1) * N_KEYS)
                gw_ref[rows, ls] = (jax.nn.gelu(at_ref[rows, ls]) * ws[k]).astype(BF16)

    half = PEER_TE // 2
    partial = []
    up_proj(0)
    for q in range(4):
        if q + 1 < 4:
            up_proj(q + 1)
        gate(q)
        if q % 2 == 1:
            ks = slice((q // 2) * half, (q // 2 + 1) * half)
            partial.append(_dot(vt_ref[0, :, ks], gw_ref[ks, :]))
    acc_ref[...] += partial[0] + partial[1]

    @pl.when(e == pl.num_programs(1) - 1)
    def _finish():
        y = DN_ALPHA * h1_ref[...] + jnp.transpose(acc_ref[...])
        out_ref[...] = _layer_norm(y, g_ref[...], b_ref[...])


def _peer_main(h1, u, vt, s2, e2, th, e1, g, b):
    t, d = h1.shape
    tm = min(512, t)
    route = pl.BlockSpec((P_HEADS, N_KEYS, tm), lambda i, e: (0, 0, i))
    return pl.pallas_call(
        _peer_main_body,
        grid=(t // tm, N_EXPERTS // PEER_TE),
        in_specs=[pl.BlockSpec((tm, d), lambda i, e: (i, 0)),
                  pl.BlockSpec((PEER_TE, d), lambda i, e: (e, 0)),
                  pl.BlockSpec((1, d, PEER_TE), lambda i, e: (e, 0, 0)),
                  route, route, route, route, _full(g.shape), _full(b.shape)],
        out_specs=pl.BlockSpec((tm, d), lambda i, e: (i, 0)),
        out_shape=jax.ShapeDtypeStruct((t, d), F32),
        scratch_shapes=[pltpu.VMEM((d, tm), BF16), pltpu.VMEM((d, tm), F32),
                        pltpu.VMEM((PEER_TE, tm), F32), pltpu.VMEM((PEER_TE, tm), BF16)],
        compiler_params=_cparams("parallel", "arbitrary"),
        name="peer_main",
    )(h1, u, vt, s2, e2, th, e1, g, b)


def _ple_body(h_ref, p_ref, gw_ref, gb_ref, pw_ref, out_ref):
    h = h_ref[...]
    gate = jax.nn.sigmoid(_dot(h.astype(BF16), gw_ref[...]) + gb_ref[...])
    out_ref[...] = h + gate * _dot(p_ref[...].astype(BF16), pw_ref[...])


def _ple(h, p, gw, gb, pw):
    t, d = h.shape
    tm = min(512, t)
    return pl.pallas_call(
        _ple_body,
        grid=(t // tm,),
        in_specs=[pl.BlockSpec((tm, d), lambda i: (i, 0)), pl.BlockSpec((tm, PLE_DIM), lambda i: (i, 0)),
                  _full(gw.shape), _full(gb.shape), _full(pw.shape)],
        out_specs=pl.BlockSpec((tm, d), lambda i: (i, 0)),
        out_shape=jax.ShapeDtypeStruct((t, d), F32),
        compiler_params=_cparams("parallel"),
        name="ple",
    )(h, p, gw, gb, pw)


def _t5_bucket(rel):
    nb = N_BUCKETS // 2
    max_exact = nb // 2
    n = jnp.abs(rel)
    large = max_exact + (jnp.log(jnp.maximum(n, 1).astype(F32) / max_exact)
                         / math.log(MAX_DISTANCE / max_exact) * (nb - max_exact)).astype(jnp.int32)
    large = jnp.minimum(large, nb - 1)
    return jnp.where(rel > 0, nb, 0) + jnp.where(n < max_exact, n, large)


def _bias_tables(tab, heads, blk):
    r = jnp.arange(blk, dtype=jnp.int32)[:, None]
    c = jnp.arange(blk, dtype=jnp.int32)[None, :]
    rel = jnp.stack([c - r, c - r - blk])
    onehot = (_t5_bucket(rel).reshape(1, -1) == jnp.arange(N_BUCKETS, dtype=jnp.int32)[:, None]).astype(F32)
    tiles = jnp.dot(tab[:, jnp.asarray(heads)].T.astype(F32), onehot, precision=lax.Precision.HIGHEST)
    tiles = tiles.reshape(len(heads) // 2, 2, 2, blk, blk)
    far = tab[_t5_bucket(jnp.int32(-(blk + 1)))][jnp.asarray(heads)]
    return tiles.astype(F32) * LOG2E, far.astype(F32) * LOG2E


def _pad_heads(w, n_heads):
    rows = w.shape[0]
    w = w.reshape(rows, n_heads, LANES // 2)
    return jnp.concatenate([w, jnp.zeros_like(w)], axis=-1).reshape(rows, n_heads * LANES)


def _swap_half(w):
    half = w.shape[-1] // 2
    return jnp.concatenate([-w[..., half:], w[..., :half]], axis=-1)


def _even_weights(w_in, w_uq, w_ukv):
    c_q, c_kv, k_r = w_in[:, 0:512], w_in[:, 512:768], w_in[:, 768:800]
    qk_d, v_d = w_in[:, 800:1824], w_in[:, 1824:2336]
    w_in2 = jnp.concatenate([c_q, c_kv, qk_d, _pad_heads(v_d, H_B), jnp.tile(k_r, (1, 4)),
                             jnp.tile(_swap_half(k_r), (1, 4))], axis=1)
    uq = w_uq.reshape(A_Q_RANK, H_A, A_NOPE + A_ROPE)
    rope = uq[:, :, A_NOPE:]
    w_uq2 = jnp.concatenate([uq[:, :, :A_NOPE].reshape(A_Q_RANK, -1), rope.reshape(A_Q_RANK, -1),
                             _swap_half(rope).reshape(A_Q_RANK, -1)], axis=1)
    ukv = w_ukv.reshape(A_KV_RANK, H_A, A_NOPE + A_V)
    w_ukv2 = jnp.concatenate([ukv[:, :, :A_NOPE].reshape(A_KV_RANK, -1),
                              _pad_heads(ukv[:, :, A_NOPE:].reshape(A_KV_RANK, -1), H_A)], axis=1)
    return w_in2.astype(BF16), w_uq2.astype(BF16), w_ukv2.astype(BF16)


_ODD_HEADS = [(2 * pp + a) * C_GROUP + g for pp in range(KV_C // 2) for g in range(C_GROUP) for a in range(2)]


def _odd_weights(w_in, w_o):
    d = w_in.shape[0]
    heads = jnp.asarray(_ODD_HEADS)
    wq = w_in[:, 0:1024].reshape(d, H_C, C_DH)[:, heads].reshape(d, -1)
    wk, wv = w_in[:, 1024:1280], w_in[:, 1280:1536]
    wqi = w_in[:, 1536:2560].reshape(d, H_IDX, IDX_DH)
    wqi_s = jnp.concatenate([_swap_half(wqi[:, :, :IDX_ROPE]), jnp.zeros_like(wqi[:, :, IDX_ROPE:])], axis=-1)
    wki = w_in[:, 2560:2624]
    wki_s = jnp.concatenate([_swap_half(wki[:, :IDX_ROPE]), jnp.zeros_like(wki[:, IDX_ROPE:])], axis=-1)
    wwi = jnp.pad(w_in[:, 2624:2640], ((0, 0), (0, LANES - H_IDX)))
    w2 = jnp.concatenate([wq, wk, _pad_heads(wv, KV_C), wqi.reshape(d, -1), wqi_s.reshape(d, -1),
                          jnp.tile(wki, (1, 2)), jnp.tile(wki_s, (1, 2)), wwi], axis=1)
    w_o2 = w_o.reshape(H_C, C_DH, -1)[heads].reshape(H_C * C_DH, -1)
    return w2.astype(BF16), w_o2.astype(BF16)


def _rope_tables(positions):
    half = A_ROPE // 2
    freqs = ROPE_THETA ** (-jnp.arange(half, dtype=F32) / half)
    ang = positions.astype(F32)[..., None] * freqs
    cos, sin = jnp.cos(ang), jnp.sin(ang)
    cos32 = jnp.concatenate([cos, cos], axis=-1)
    sin32 = jnp.concatenate([sin, sin], axis=-1)
    ones, zeros = jnp.ones_like(cos32), jnp.zeros_like(sin32)
    return (jnp.tile(cos32, (1, 1, 4)), jnp.tile(sin32, (1, 1, 4)),
            jnp.tile(jnp.concatenate([cos32, ones], axis=-1), (1, 1, 2)),
            jnp.tile(jnp.concatenate([sin32, zeros], axis=-1), (1, 1, 2)))


def kernel(x, p, positions, rel_bias, ev_w_in, ev_w_uq, ev_w_ukv, ev_q_norm, ev_kv_norm, ev_lam_q1, ev_lam_k1, ev_lam_q2, ev_lam_k2, ev_subln, ev_w_o, od_w_in, od_w_o, ln1_g, ln1_b, ln2_g, ln2_b, peer_w_q, peer_k1, peer_k2, peer_u, peer_v, ple_w, ple_gate_w, ple_gate_b):
    b, s, d = x.shape
    t = b * s
    assert s % ATT_BLK == 0 and d == D_MODEL
    n_sel = min(TOPK_MAX, s // 4)
    cos_f, sin_f, cos_p, sin_p = _rope_tables(positions)
    bias_b, bfar_b = _bias_tables(rel_bias[:, :H_B], list(range(H_B)), ATT_BLK)
    bias_c, bfar_c = _bias_tables(rel_bias[:, H_B:], _ODD_HEADS, ATT_BLK)
    row = lambda a: a.reshape(1, -1).astype(F32)

    h = x
    for i in range(DEPTH):
        j = i // 2
        if i % 2 == 0:
            lam_init = 0.8 - 0.6 * math.exp(-0.3 * i)
            lam = (jnp.exp(jnp.sum(ev_lam_q1[j] * ev_lam_k1[j], dtype=F32))
                   - jnp.exp(jnp.sum(ev_lam_q2[j] * ev_lam_k2[j], dtype=F32)) + lam_init).reshape(1)
            w_in2, w_uq2, w_ukv2 = _even_weights(ev_w_in[j], ev_w_uq[j], ev_w_ukv[j])
            qn, qr, kn, kr, va, qd, kd, vd = _even_proj(h, cos_f, sin_f, w_in2, w_uq2, w_ukv2,
                                                        row(ev_q_norm[j]), row(ev_kv_norm[j]))
            sg = row(jnp.tile(ev_subln[j], 2))
            oa, od = _even_attn(lam, bfar_b, qn, qr, kn, kr, va, qd, kd, vd, bias_b, sg, lam_init)
            w_o = ev_w_o[j].astype(BF16)
            os_ = [oa.reshape(t, -1), od.reshape(t, -1)]
            ws = [w_o[:H_A * A_V], w_o[H_A * A_V:]]
        else:
            w_in2, w_o2 = _odd_weights(od_w_in[j], od_w_o[j])
            q, k, v, qi, ki, wi = _odd_proj(h, cos_p, sin_p, w_in2)
            o = _dsa_attn(bfar_c, qi, ki, wi, q, k, v, bias_c, n_sel)
            os_, ws = [o.reshape(t, -1)], [w_o2]
        h1 = _oproj_ln(os_, ws, h.reshape(t, d), row(ln1_g[i]), row(ln1_b[i]))
        s2, e2, th, e1 = _peer_route(h1, peer_w_q[i].astype(BF16), peer_k1[i].astype(BF16),
                                     peer_k2[i].astype(BF16))
        vt = jnp.transpose(peer_v[i].reshape(N_EXPERTS // PEER_TE, PEER_TE, d), (0, 2, 1)).astype(BF16)
        h2 = _peer_main(h1, peer_u[i].astype(BF16), vt,
                        s2, e2, th, e1, row(ln2_g[i]), row(ln2_b[i]))
        h = _ple(h2, p[i].reshape(t, PLE_DIM), ple_gate_w[i].astype(BF16), row(ple_gate_b[i]),
                 ple_w[i].astype(BF16)).reshape(b, s, d)
    return h
```
